```python
import math
import jax
import jax.numpy as jnp
from jax import lax
import numpy as np


D_MODEL = 2048
BATCH = 1
SEQ = 8192
DEPTH = 4

EPS = 1e-6
D_FF = 5632
D_MIX = 2 * D_MODEL
CONV_WIDTH = 5
SSD_HEADS = 32
SSD_HEAD_DIM = 64
SSD_WIDTH = SSD_HEADS * SSD_HEAD_DIM
SSD_GROUPS = 4
SSD_STATE = 128
SSD_CHUNK = 128
SSD_BC = SSD_GROUPS * SSD_STATE
SSD_CONV_CH = SSD_WIDTH + 2 * SSD_BC
SSD_IN = SSD_WIDTH + SSD_CONV_CH + 2 * SSD_HEADS
GDN_HEADS = 16
GDN_HEAD_DIM = 128
GDN_WIDTH = GDN_HEADS * GDN_HEAD_DIM
GDN_CHUNK = 64
GDN_CONV_CH = 3 * GDN_WIDTH
GDN_IN = GDN_CONV_CH + GDN_WIDTH + 4 * GDN_HEADS
D_IN_PROJ = SSD_IN + GDN_IN

kernel_name = 'hybrid_ssd_gdn_macaron_encoder'


def rms_norm(x, g):
    xf = x.astype(jnp.float32)
    xf = xf * lax.rsqrt(jnp.mean(xf * xf, axis=-1, keepdims=True) + EPS)
    return (xf * g.astype(jnp.float32)).astype(x.dtype)


def group_rms(x, n_groups):
    shp = x.shape
    xg = x.reshape(shp[:-1] + (n_groups, shp[-1] // n_groups))
    xg = xg * lax.rsqrt(jnp.mean(xg * xg, axis=-1, keepdims=True) + EPS)
    return xg.reshape(shp)


def l2_normalise(x):
    return x * lax.rsqrt(jnp.sum(x * x, axis=-1, keepdims=True) + EPS)


def swiglu(h, w_gu, w_down):
    gate, up = jnp.split(h @ w_gu, 2, axis=-1)
    return (jax.nn.silu(gate) * up) @ w_down


def centred_dwconv(x, w):
    pad = w.shape[0] // 2
    return lax.conv_general_dilated(
        x, w[:, None, :].astype(x.dtype), window_strides=(1,), padding=[(pad, pad)],
        dimension_numbers=('NWC', 'WIO', 'NWC'), feature_group_count=x.shape[-1])


def flip_seq(t):
    return jnp.flip(t, axis=1)


def tril_segsum(a):
    t = a.shape[-1]
    cs = jnp.cumsum(a, axis=-1)
    diff = cs[..., :, None] - cs[..., None, :]
    return jnp.where(jnp.tril(jnp.ones((t, t), dtype=bool)), diff, -jnp.inf)


def ssd_scan(x, dt, a_neg, bm, cm):
    b, l, h, p = x.shape
    g, n = bm.shape[-2:]
    r = h // g
    q = SSD_CHUNK
    c = l // q
    x_dt = (x * dt[..., None]).reshape(b, c, q, g, r, p)
    a = (dt * a_neg).reshape(b, c, q, g, r).transpose(0, 3, 4, 1, 2)
    bm = bm.reshape(b, c, q, g, n)
    cm = cm.reshape(b, c, q, g, n)
    a_cs = jnp.cumsum(a, axis=-1)
    decay_in = jnp.exp(tril_segsum(a))
    cb = jnp.einsum('bclgn,bcsgn->bcgls', cm, bm)
    y_diag = jnp.einsum('bcgls,bgrcls,bcsgrp->bclgrp', cb, decay_in, x_dt)
    decay_to_end = jnp.exp(a_cs[..., -1:] - a_cs)
    states = jnp.einsum('bcsgn,bgrcs,bcsgrp->bcgrpn', bm, decay_to_end, x_dt)
    chunk_decay = jnp.exp(a_cs[..., -1])

    def step(carry, inp):
        st, dec = inp
        return carry * dec[..., None, None] + st, carry

    h0 = jnp.zeros_like(states[:, 0])
    _, h_prev = lax.scan(step, h0, (jnp.moveaxis(states, 1, 0), jnp.moveaxis(chunk_decay, 3, 0)))
    h_prev = jnp.moveaxis(h_prev, 0, 1)
    y_off = jnp.einsum('bclgn,bcgrpn,bgrcl->bclgrp', cm, h_prev, jnp.exp(a_cs))
    return (y_diag + y_off).reshape(b, l, h, p)


def gated_delta_chunked(q, k, v, g, beta):
    b, l, h, dk = q.shape
    dv = v.shape[-1]
    cs = GDN_CHUNK
    nc = l // cs

    def to_chunks(t):
        return t.reshape(b, nc, cs, h, -1).transpose(0, 3, 1, 2, 4)

    q = to_chunks(q) * (dk ** -0.5)
    k = to_chunks(k)
    v = to_chunks(v)
    beta = beta.reshape(b, nc, cs, h).transpose(0, 3, 1, 2)
    g_cs = jnp.cumsum(g.reshape(b, nc, cs, h).transpose(0, 3, 1, 2), axis=-1)
    decay = jnp.exp(tril_segsum_from_cumsum(g_cs))
    strict = jnp.tril(jnp.ones((cs, cs), dtype=bool), -1)
    k_beta = k * beta[..., None]
    a_strict = jnp.where(strict, jnp.einsum('bhnid,bhnjd->bhnij', k_beta, k) * decay, 0.0)
    rhs = jnp.concatenate([v * beta[..., None], k_beta * jnp.exp(g_cs)[..., None]], axis=-1)
    sol = lax.linalg.triangular_solve(a_strict, rhs, left_side=True, lower=True, unit_diagonal=True)
    u, w = sol[..., :dv], sol[..., dv:]
    attn_qk = jnp.einsum('bhnid,bhnjd->bhnij', q, k) * decay
    q_decay = q * jnp.exp(g_cs)[..., None]
    k_to_end = k * jnp.exp(g_cs[..., -1:] - g_cs)[..., None]
    last_decay = jnp.exp(g_cs[..., -1])

    def step(s, inp):
        u_c, w_c, qd_c, aq_c, ke_c, ld_c = inp
        v_new = u_c - jnp.einsum('bhid,bhde->bhie', w_c, s)
        o = jnp.einsum('bhid,bhde->bhie', qd_c, s) + jnp.einsum('bhij,bhje->bhie', aq_c, v_new)
        s = s * ld_c[..., None, None] + jnp.einsum('bhid,bhie->bhde', ke_c, v_new)
        return s, o

    s0 = jnp.zeros((b, h, dk, dv), q.dtype)
    xs = (jnp.moveaxis(u, 2, 0), jnp.moveaxis(w, 2, 0), jnp.moveaxis(q_decay, 2, 0),
          jnp.moveaxis(attn_qk, 2, 0), jnp.moveaxis(k_to_end, 2, 0), jnp.moveaxis(last_decay, 2, 0))
    _, o = lax.scan(step, s0, xs)
    return o.transpose(1, 0, 3, 2, 4).reshape(b, l, h, dv)


def tril_segsum_from_cumsum(cs_vals):
    t = cs_vals.shape[-1]
    diff = cs_vals[..., :, None] - cs_vals[..., None, :]
    return jnp.where(jnp.tril(jnp.ones((t, t), dtype=bool)), diff, -jnp.inf)


def hybrid_mixer(h, w_in, ssd_conv_w, ssd_conv_b, ssd_dt_bias, ssd_a_log, ssd_d, ssd_norm_g,
                 gdn_conv_w, gdn_dt_bias, gdn_a_log, gdn_norm_g, w_out):
    b, l, _ = h.shape
    f32 = jnp.float32
    proj = h @ w_in
    ssd_in, gdn_in = proj[..., :SSD_IN], proj[..., SSD_IN:]

    z_s, xbc, dt_raw = jnp.split(ssd_in, [SSD_WIDTH, SSD_WIDTH + SSD_CONV_CH], axis=-1)
    xbc = jax.nn.silu(centred_dwconv(xbc, ssd_conv_w) + ssd_conv_b.astype(xbc.dtype)).astype(f32)
    xs, bm, cm = jnp.split(xbc, [SSD_WIDTH, SSD_WIDTH + SSD_BC], axis=-1)
    xs = xs.reshape(b, l, SSD_HEADS, SSD_HEAD_DIM)
    bm = bm.reshape(b, l, SSD_GROUPS, SSD_STATE)
    cm = cm.reshape(b, l, SSD_GROUPS, SSD_STATE)
    dt = jax.nn.softplus(dt_raw.astype(f32).reshape(b, l, 2, SSD_HEADS) + ssd_dt_bias.astype(f32))
    a_neg = -jnp.exp(ssd_a_log.astype(f32))
    y_fwd = ssd_scan(xs, dt[:, :, 0], a_neg[0], bm, cm)
    y_bwd = flip_seq(ssd_scan(flip_seq(xs), flip_seq(dt[:, :, 1]), a_neg[1], flip_seq(bm), flip_seq(cm)))
    y = (y_fwd + y_bwd + ssd_d.astype(f32)[:, None] * xs).reshape(b, l, SSD_WIDTH)
    y_ssd = group_rms(y * jax.nn.silu(z_s.astype(f32)), SSD_GROUPS) * ssd_norm_g.astype(f32)

    qkv, z_g, ab = jnp.split(gdn_in, [GDN_CONV_CH, GDN_CONV_CH + GDN_WIDTH], axis=-1)
    qkv = jax.nn.silu(centred_dwconv(qkv, gdn_conv_w)).astype(f32)
    q, k, v = jnp.split(qkv, 3, axis=-1)
    q = l2_normalise(q.reshape(b, l, GDN_HEADS, GDN_HEAD_DIM))
    k = l2_normalise(k.reshape(b, l, GDN_HEADS, GDN_HEAD_DIM))
    v = v.reshape(b, l, GDN_HEADS, GDN_HEAD_DIM)
    ab = ab.astype(f32).reshape(b, l, 2, 2, GDN_HEADS)
    g = -jnp.exp(gdn_a_log.astype(f32)) * jax.nn.softplus(ab[:, :, 0] + gdn_dt_bias.astype(f32))
    beta = jax.nn.sigmoid(ab[:, :, 1])
    o_fwd = gated_delta_chunked(q, k, v, g[:, :, 0], beta[:, :, 0])
    o_bwd = flip_seq(gated_delta_chunked(flip_seq(q), flip_seq(k), flip_seq(v),
                                         flip_seq(g[:, :, 1]), flip_seq(beta[:, :, 1])))
    o = group_rms(o_fwd + o_bwd, 1) * gdn_norm_g.astype(f32)
    o = o * jax.nn.silu(z_g.astype(f32).reshape(b, l, GDN_HEADS, GDN_HEAD_DIM))
    y_gdn = o.reshape(b, l, GDN_WIDTH)

    y_all = jnp.concatenate([y_ssd, y_gdn], axis=-1).astype(h.dtype)
    return y_all @ w_out


def setup_inputs(seed: int = 0) -> dict:
    key = jax.random.key(seed)
    ks = jax.random.split(key, 23)
    f32 = jnp.float32

    def dense(k, fan_in, fan_out):
        return jax.random.normal(k, (DEPTH, fan_in, fan_out), f32) * fan_in ** -0.5

    def gain(k, d):
        return 1.0 + 0.02 * jax.random.normal(k, (DEPTH, d), f32)

    def a_log(k, nh):
        return jnp.log(jax.random.uniform(k, (DEPTH, 2, nh), f32, 1.0, 16.0))

    def dt_bias(k, nh):
        dt = jnp.exp(jax.random.uniform(k, (DEPTH, 2, nh), f32, math.log(1e-3), math.log(1e-1)))
        return dt + jnp.log(-jnp.expm1(-dt))

    return {
        'x': jax.random.normal(ks[0], (BATCH, SEQ, D_MODEL), f32),
        'ffn1_pre_g': gain(ks[1], D_MODEL),
        'ffn1_w_gu': dense(ks[2], D_MODEL, 2 * D_FF),
        'ffn1_w_down': dense(ks[3], D_FF, D_MODEL),
        'ffn1_post_g': gain(ks[4], D_MODEL),
        'mix_pre_g': gain(ks[5], D_MODEL),
        'w_in': dense(ks[6], D_MODEL, D_IN_PROJ),
        'ssd_conv_w': jax.random.normal(ks[7], (DEPTH, CONV_WIDTH, SSD_CONV_CH), f32) * CONV_WIDTH ** -0.5,
        'ssd_conv_b': 0.02 * jax.random.normal(ks[8], (DEPTH, SSD_CONV_CH), f32),
        'ssd_dt_bias': dt_bias(ks[9], SSD_HEADS),
        'ssd_a_log': a_log(ks[10], SSD_HEADS),
        'ssd_d': 1.0 + 0.1 * jax.random.normal(ks[11], (DEPTH, SSD_HEADS), f32),
        'ssd_norm_g': gain(ks[12], SSD_WIDTH),
        'gdn_conv_w': jax.random.normal(ks[13], (DEPTH, CONV_WIDTH, GDN_CONV_CH), f32) * CONV_WIDTH ** -0.5,
        'gdn_dt_bias': dt_bias(ks[14], GDN_HEADS),
        'gdn_a_log': a_log(ks[15], GDN_HEADS),
        'gdn_norm_g': gain(ks[16], GDN_HEAD_DIM),
        'w_out': dense(ks[17], D_MIX, D_MODEL),
        'mix_post_g': gain(ks[18], D_MODEL),
        'ffn2_pre_g': gain(ks[19], D_MODEL),
        'ffn2_w_gu': dense(ks[20], D_MODEL, 2 * D_FF),
        'ffn2_w_down': dense(ks[21], D_FF, D_MODEL),
        'ffn2_post_g': gain(ks[22], D_MODEL),
    }


def reference(x, ffn1_pre_g, ffn1_w_gu, ffn1_w_down, ffn1_post_g, mix_pre_g, w_in,
              ssd_conv_w, ssd_conv_b, ssd_dt_bias, ssd_a_log, ssd_d, ssd_norm_g,
              gdn_conv_w, gdn_dt_bias, gdn_a_log, gdn_norm_g, w_out, mix_post_g,
              ffn2_pre_g, ffn2_w_gu, ffn2_w_down, ffn2_post_g):
    for i in range(DEPTH):
        f = swiglu(rms_norm(x, ffn1_pre_g[i]), ffn1_w_gu[i], ffn1_w_down[i])
        x = x + 0.5 * rms_norm(f, ffn1_post_g[i])
        m = hybrid_mixer(rms_norm(x, mix_pre_g[i]), w_in[i],
                         ssd_conv_w[i], ssd_conv_b[i], ssd_dt_bias[i], ssd_a_log[i], ssd_d[i], ssd_norm_g[i],
                         gdn_conv_w[i], gdn_dt_bias[i], gdn_a_log[i], gdn_norm_g[i], w_out[i])
        x = x + rms_norm(m, mix_post_g[i])
        f = swiglu(rms_norm(x, ffn2_pre_g[i]), ffn2_w_gu[i], ffn2_w_down[i])
        x = x + 0.5 * rms_norm(f, ffn2_post_g[i])
    return x
```

```python
import functools

import jax
import jax.numpy as jnp
from jax import lax
from jax.experimental import pallas as pl
from jax.experimental.pallas import tpu as pltpu

F32 = jnp.float32
BF16 = jnp.bfloat16
HIGHEST = lax.Precision.HIGHEST

D_MODEL = 2048
SEQ = 8192
DEPTH = 4
EPS = 1e-6
D_FF = 5632
CONV_WIDTH = 5
SSD_HEADS = 32
SSD_HEAD_DIM = 64
SSD_WIDTH = 2048
SSD_GROUPS = 4
SSD_STATE = 128
SSD_CHUNK = 128
SSD_BC = 512
SSD_CONV_CH = SSD_WIDTH + 2 * SSD_BC
GDN_HEADS = 16
GDN_HEAD_DIM = 128
GDN_WIDTH = 2048
GDN_CHUNK = 64
GDN_CONV_CH = 3 * GDN_WIDTH
D_IN_PROJ = 13440

LANES = 128
COL_ZS = 0
COL_ZG = 2048
COL_CONV = 4096
N_CONV = SSD_CONV_CH + GDN_CONV_CH
COL_GATE = COL_CONV + N_CONV
N_CONV_BLK = N_CONV // LANES
G_SSD = 0
G_GDN = 64
G_BETA = 96

GDN_BLOCK = 256
VMEM_LIMIT = 56 * 1024 * 1024


def _mm(a, b, precision=None):
    return jnp.dot(a, b, preferred_element_type=F32, precision=precision)


def _nt(a, b, precision=None):
    return lax.dot_general(a, b, (((1,), (1,)), ((), ())), preferred_element_type=F32, precision=precision)


def _tn(a, b):
    return lax.dot_general(a, b, (((0,), (0,)), ((), ())), preferred_element_type=F32)


def _rms(x, g):
    return x * lax.rsqrt(jnp.mean(x * x, axis=-1, keepdims=True) + EPS) * g


def _silu(x):
    return x * jax.nn.sigmoid(x)


def _softplus(x):
    return jnp.maximum(x, 0.0) + jnp.log1p(jnp.exp(-jnp.abs(x)))


def _params(sem):
    return pltpu.CompilerParams(dimension_semantics=sem, vmem_limit_bytes=VMEM_LIMIT)


def _ffn_kernel(l_ref, x_ref, pre_ref, wg_ref, wu_ref, wd_ref, post_ref, o_ref, h_ref, acc_ref, *, nj):
    j = pl.program_id(1)

    @pl.when(j == 0)
    def _():
        h_ref[...] = _rms(x_ref[...], pre_ref[...]).astype(BF16)
        acc_ref[...] = jnp.zeros_like(acc_ref)

    h = h_ref[...]
    gate = _mm(h, wg_ref[...])
    up = _mm(h, wu_ref[...])
    act = (_silu(gate) * up).astype(BF16)
    acc_ref[...] += _mm(act, wd_ref[...])

    @pl.when(j == nj - 1)
    def _():
        o_ref[...] = x_ref[...] + 0.5 * _rms(acc_ref[...], post_ref[...])


def _ffn_call(layer, x, pre_g, w_gu, w_down, post_g, *, tm=512, tf=512):
    seq, d = x.shape
    nj = D_FF // tf
    gs = pltpu.PrefetchScalarGridSpec(
        num_scalar_prefetch=1,
        grid=(seq // tm, nj),
        in_specs=[
            pl.BlockSpec((tm, d), lambda i, j, l: (i, 0)),
            pl.BlockSpec((None, 1, d), lambda i, j, l: (l[0], 0, 0)),
            pl.BlockSpec((None, d, tf), lambda i, j, l: (l[0], 0, j)),
            pl.BlockSpec((None, d, tf), lambda i, j, l: (l[0], 0, j + nj)),
            pl.BlockSpec((None, tf, d), lambda i, j, l: (l[0], j, 0)),
            pl.BlockSpec((None, 1, d), lambda i, j, l: (l[0], 0, 0)),
        ],
        out_specs=pl.BlockSpec((tm, d), lambda i, j, l: (i, 0)),
        scratch_shapes=[pltpu.VMEM((tm, d), BF16), pltpu.VMEM((tm, d), F32)],
    )
    return pl.pallas_call(
        functools.partial(_ffn_kernel, nj=nj),
        grid_spec=gs,
        out_shape=jax.ShapeDtypeStruct((seq, d), F32),
        compiler_params=_params(("parallel", "arbitrary")),
        name="ffn",
    )(layer, x, pre_g, w_gu, w_gu, w_down, post_g)


def _inproj_kernel(l_ref, x_ref, pre_ref, w_ref, o_ref, h_ref):
    @pl.when(pl.program_id(1) == 0)
    def _():
        h_ref[...] = _rms(x_ref[...], pre_ref[...]).astype(BF16)

    o_ref[...] = _mm(h_ref[...], w_ref[...])


def _inproj_call(layer, x, pre_g, w_in, *, tm=512, tn=1920):
    seq, d = x.shape
    gs = pltpu.PrefetchScalarGridSpec(
        num_scalar_prefetch=1,
        grid=(seq // tm, D_IN_PROJ // tn),
        in_specs=[
            pl.BlockSpec((tm, d), lambda i, j, l: (i, 0)),
            pl.BlockSpec((None, 1, d), lambda i, j, l: (l[0], 0, 0)),
            pl.BlockSpec((None, d, tn), lambda i, j, l: (l[0], 0, j)),
        ],
        out_specs=pl.BlockSpec((tm, tn), lambda i, j, l: (i, j)),
        scratch_shapes=[pltpu.VMEM((tm, d), BF16)],
    )
    return pl.pallas_call(
        _inproj_kernel,
        grid_spec=gs,
        out_shape=jax.ShapeDtypeStruct((seq, D_IN_PROJ), F32),
        compiler_params=_params(("parallel", "arbitrary")),
        name="inproj",
    )(layer, x, pre_g, w_in)


def _gate_kernel(l_ref, raw_ref, rawt_ref, brow_ref, arow_ref, bcol_ref, acol_ref,
                 c1_ref, c2_ref, c3_ref, r2_ref):
    q = SSD_CHUNK
    raw = raw_ref[...]
    lane = lax.broadcasted_iota(jnp.int32, (q, LANES), 1)
    sp = _softplus(raw + brow_ref[...])
    c1_ref[...] = jnp.where(lane < G_BETA, sp, jax.nn.sigmoid(raw))
    a = jnp.where(lane < G_BETA, sp * (-jnp.exp(arow_ref[...])), 0.0)

    ri = lax.broadcasted_iota(jnp.int32, (q, q), 0)
    ci = lax.broadcasted_iota(jnp.int32, (q, q), 1)
    same = (ri >> 6) == (ci >> 6)
    lo128 = (ri >= ci).astype(F32)
    up128 = (ri <= ci).astype(F32)
    lo64 = jnp.where(same, lo128, 0.0)
    up64 = jnp.where(same, up128, 0.0)
    ones = jnp.ones((q, q), F32)
    same_f = same.astype(F32)

    c2 = jnp.where(lane < G_SSD + SSD_HEADS, _mm(lo128, a, HIGHEST),
                   jnp.where(lane < G_GDN, _mm(up128, a, HIGHEST),
                             jnp.where(lane < G_GDN + GDN_HEADS, _mm(lo64, a, HIGHEST),
                                       _mm(up64, a, HIGHEST))))
    c2_ref[...] = c2
    c3_ref[...] = jnp.where(lane < G_GDN, _mm(ones, a, HIGHEST), _mm(same_f, a, HIGHEST))

    sub = lax.broadcasted_iota(jnp.int32, (LANES, q), 0)
    at = jnp.where(sub < G_BETA, _softplus(rawt_ref[...] + bcol_ref[...]) * (-jnp.exp(acol_ref[...])), 0.0)
    r2_ref[...] = jnp.where(sub < G_SSD + SSD_HEADS, _nt(at, lo128, HIGHEST),
                            jnp.where(sub < G_GDN, _nt(at, up128, HIGHEST),
                                      jnp.where(sub < G_GDN + GDN_HEADS, _nt(at, lo64, HIGHEST),
                                                _nt(at, up64, HIGHEST))))


def _gate_call(layer, proj, gate_t, brow, arow, bcol, acol):
    seq = proj.shape[0]
    q = SSD_CHUNK
    row_p = pl.BlockSpec((None, 1, LANES), lambda i, l: (l[0], 0, 0))
    col_p = pl.BlockSpec((None, LANES, 1), lambda i, l: (l[0], 0, 0))
    colform = pl.BlockSpec((q, LANES), lambda i, l: (i, 0))
    gs = pltpu.PrefetchScalarGridSpec(
        num_scalar_prefetch=1,
        grid=(seq // q,),
        in_specs=[
            pl.BlockSpec((q, LANES), lambda i, l: (i, COL_GATE // LANES)),
            pl.BlockSpec((LANES, q), lambda i, l: (0, i)),
            row_p, row_p, col_p, col_p,
        ],
        out_specs=[colform, colform, colform, pl.BlockSpec((LANES, q), lambda i, l: (0, i))],
    )
    col_shape = jax.ShapeDtypeStruct((seq, LANES), F32)
    return pl.pallas_call(
        _gate_kernel,
        grid_spec=gs,
        out_shape=[col_shape, col_shape, col_shape, jax.ShapeDtypeStruct((LANES, seq), F32)],
        compiler_params=_params(("parallel",)),
        name="gateprep",
    )(layer, proj, gate_t, brow, arow, bcol, acol)


def _conv_kernel(l_ref, prev_ref, cur_ref, next_ref, w_ref, b_ref, o_ref, ext_ref, *, t, ni):
    i = pl.program_id(0)
    halo = 8
    ext_ref[0:halo, :] = jnp.where(i > 0, prev_ref[...], 0.0)
    ext_ref[halo:halo + t, :] = cur_ref[...]
    ext_ref[halo + t:2 * halo + t, :] = jnp.where(i < ni - 1, next_ref[...], 0.0)
    pad = CONV_WIDTH // 2
    y = b_ref[...]
    for j in range(CONV_WIDTH):
        y = y + w_ref[j:j + 1, :] * ext_ref[halo - pad + j:halo - pad + j + t, :]
    y = _silu(y).astype(BF16)
    for s in range(o_ref.shape[0]):
        o_ref[s] = y[:, s * LANES:(s + 1) * LANES]


def _conv_call(layer, proj, conv_w, conv_b, *, t=512, tc=1024):
    seq = proj.shape[0]
    ni = seq // t
    coff = COL_CONV // tc
    nblk = tc // LANES
    gs = pltpu.PrefetchScalarGridSpec(
        num_scalar_prefetch=1,
        grid=(ni, N_CONV // tc),
        in_specs=[
            pl.BlockSpec((8, tc), lambda i, j, l: (jnp.maximum(i * (t // 8) - 1, 0), coff + j)),
            pl.BlockSpec((t, tc), lambda i, j, l: (i, coff + j)),
            pl.BlockSpec((8, tc), lambda i, j, l: (jnp.minimum((i + 1) * (t // 8), seq // 8 - 1), coff + j)),
            pl.BlockSpec((None, CONV_WIDTH, tc), lambda i, j, l: (l[0], 0, j)),
            pl.BlockSpec((None, 1, tc), lambda i, j, l: (l[0], 0, j)),
        ],
        out_specs=pl.BlockSpec((nblk, t, LANES), lambda i, j, l: (j, i, 0)),
        scratch_shapes=[pltpu.VMEM((t + 16, tc), F32)],
    )
    return pl.pallas_call(
        functools.partial(_conv_kernel, t=t, ni=ni),
        grid_spec=gs,
        out_shape=jax.ShapeDtypeStruct((N_CONV_BLK, seq, LANES), BF16),
        compiler_params=_params(("parallel", "parallel")),
        name="conv",
    )(layer, proj, proj, proj, conv_w, conv_b)


def _expand(x, e):
    hi = x.astype(BF16)
    lo = (x - hi.astype(F32)).astype(BF16)
    n = x.shape[0]
    r = _mm(jnp.concatenate([hi, lo], axis=0), e)
    return r[:n] + r[n:]


def _ssd_dir(x_ref, b_ref, c_ref, c1_ref, c2_ref, c3_ref, r2_ref, s_ref, y_ref, *, d):
    q = SSD_CHUNK
    gw = SSD_WIDTH // SSD_GROUPS
    hpg = SSD_HEADS // SSD_GROUPS
    dt_all = c1_ref[...]
    cs_all = c2_ref[...]
    tot_all = c3_ref[...]
    stacked = jnp.concatenate(
        [dt_all, jnp.exp(cs_all), jnp.exp(tot_all - cs_all), jnp.exp(tot_all[0:16])], axis=0)
    ri = lax.broadcasted_iota(jnp.int32, (q, q), 0)
    ci = lax.broadcasted_iota(jnp.int32, (q, q), 1)
    mask = (ri >= ci) if d == 0 else (ri <= ci)
    kk = lax.broadcasted_iota(jnp.int32, (LANES, gw), 0)
    nn = lax.broadcasted_iota(jnp.int32, (LANES, gw), 1)
    lane = lax.broadcasted_iota(jnp.int32, (q, LANES), 1)
    for g in range(SSD_GROUPS):
        base = G_SSD + SSD_HEADS * d + hpg * g
        sel = (kk == base + (nn >> 6)).astype(BF16)
        ex = _expand(stacked, sel)
        dt_e, ecs_e, dte_e, etot_e = ex[0:q], ex[q:2 * q], ex[2 * q:3 * q], ex[3 * q:3 * q + 1]
        xg = jnp.concatenate([x_ref[4 * g + p] for p in range(4)], axis=1).astype(F32)
        xdt = xg * dt_e
        xdt16 = xdt.astype(BF16)
        xdte16 = (xdt * dte_e).astype(BF16)
        bg = b_ref[g]
        cg = c_ref[g]
        cb = _nt(cg, bg)
        s_prev = s_ref[d, g]
        y_off = _mm(cg, s_prev.astype(BF16)) * ecs_e
        s_ref[d, g] = s_prev * etot_e + _tn(bg, xdte16)
        ys = []
        for pair in range(hpg // 2):
            ms = []
            for hh in (2 * pair, 2 * pair + 1):
                r = base + hh
                dec = jnp.exp(jnp.where(mask, cs_all[:, r:r + 1] - r2_ref[r:r + 1, :], -jnp.inf))
                ms.append((cb * dec).astype(BF16))
            lhs = jnp.concatenate(ms, axis=1)
            xp = xdt16[:, pair * LANES:(pair + 1) * LANES]
            zero = jnp.zeros_like(xp)
            rhs = jnp.concatenate([jnp.where(lane < SSD_HEAD_DIM, xp, zero),
                                   jnp.where(lane >= SSD_HEAD_DIM, xp, zero)], axis=0)
            ys.append(_mm(lhs, rhs))
        y = jnp.concatenate(ys, axis=1) + y_off
        y_ref[:, g * gw:(g + 1) * gw] = y.astype(BF16)


def _ssd_kernel(l_ref,
                xf_ref, bf_ref, cf_ref, c1f_ref, c2f_ref, c3f_ref, r2f_ref,
                xb_ref, bb_ref, cb_ref, c1b_ref, c2b_ref, c3b_ref, r2b_ref,
                yf_ref, yb_ref, s_ref):
    @pl.when(pl.program_id(0) == 0)
    def _():
        s_ref[...] = jnp.zeros_like(s_ref)

    _ssd_dir(xf_ref, bf_ref, cf_ref, c1f_ref, c2f_ref, c3f_ref, r2f_ref, s_ref, yf_ref, d=0)
    _ssd_dir(xb_ref, bb_ref, cb_ref, c1b_ref, c2b_ref, c3b_ref, r2b_ref, s_ref, yb_ref, d=1)


def _ssd_call(layer, act, c1, c2, c3, r2):
    seq = c1.shape[0]
    q = SSD_CHUNK
    nc = seq // q

    def dir_specs(rowblk):
        return [
            pl.BlockSpec((16, q, LANES), lambda i, l: (0, rowblk(i), 0)),
            pl.BlockSpec((4, q, LANES), lambda i, l: (4, rowblk(i), 0)),
            pl.BlockSpec((4, q, LANES), lambda i, l: (5, rowblk(i), 0)),
            pl.BlockSpec((q, LANES), lambda i, l: (rowblk(i), 0)),
            pl.BlockSpec((q, LANES), lambda i, l: (rowblk(i), 0)),
            pl.BlockSpec((q, LANES), lambda i, l: (rowblk(i), 0)),
            pl.BlockSpec((LANES, q), lambda i, l: (0, rowblk(i))),
        ]

    fwd = lambda i: i
    bwd = lambda i: nc - 1 - i
    gs = pltpu.PrefetchScalarGridSpec(
        num_scalar_prefetch=1,
        grid=(nc,),
        in_specs=dir_specs(fwd) + dir_specs(bwd),
        out_specs=[pl.BlockSpec((q, SSD_WIDTH), lambda i, l: (fwd(i), 0)),
                   pl.BlockSpec((q, SSD_WIDTH), lambda i, l: (bwd(i), 0))],
        scratch_shapes=[pltpu.VMEM((2, SSD_GROUPS, SSD_STATE, SSD_WIDTH // SSD_GROUPS), F32)],
    )
    y_shape = jax.ShapeDtypeStruct((seq, SSD_WIDTH), BF16)
    args = (act, act, act, c1, c2, c3, r2)
    return pl.pallas_call(
        _ssd_kernel,
        grid_spec=gs,
        out_shape=[y_shape, y_shape],
        compiler_params=_params(("arbitrary",)),
        name="ssd",
    )(layer, *args, *args)


def _gdn_stream(q_ref, k_ref, v_ref, c1_ref, c2_ref, c3_ref, r2_ref, s_ref, o_ref, *, j, d, head):
    r = GDN_BLOCK
    c = GDN_CHUNK
    dh = GDN_HEAD_DIM
    q = q_ref[j].astype(F32)
    k = k_ref[j].astype(F32)
    v = v_ref[j].astype(F32)
    q = q * (lax.rsqrt(jnp.sum(q * q, axis=-1, keepdims=True) + EPS) * (dh ** -0.5))
    k = k * lax.rsqrt(jnp.sum(k * k, axis=-1, keepdims=True) + EPS)

    lane = lax.broadcasted_iota(jnp.int32, (r, LANES), 1)
    ig = G_GDN + GDN_HEADS * d + head
    ib = G_BETA + GDN_HEADS * d + head
    gcs = jnp.sum(jnp.where(lane == ig, c2_ref[...], 0.0), axis=-1, keepdims=True)
    gtot = jnp.sum(jnp.where(lane == ig, c3_ref[...], 0.0), axis=-1, keepdims=True)
    beta = jnp.sum(jnp.where(lane == ib, c1_ref[...], 0.0), axis=-1, keepdims=True)
    grow = r2_ref[pl.ds(ig, 1), :]

    ri = lax.broadcasted_iota(jnp.int32, (r, r), 0)
    ci = lax.broadcasted_iota(jnp.int32, (r, r), 1)
    same = (ri >> 6) == (ci >> 6)
    if d == 0:
        incl = same & (ri >= ci)
        strict = same & (ri > ci)
    else:
        incl = same & (ri <= ci)
        strict = same & (ri < ci)
    decay = jnp.exp(jnp.where(incl, gcs - grow, -jnp.inf))

    kb = k * beta
    k16 = k.astype(BF16)
    kb16 = kb.astype(BF16)
    q16 = q.astype(BF16)
    p = jnp.where(strict, -(_nt(kb16, k16) * decay), 0.0)
    attn16 = (_nt(q16, k16) * decay).astype(BF16)
    eg = jnp.exp(gcs)
    rhs16 = jnp.concatenate([v * beta, kb * eg], axis=1).astype(BF16)

    t = jnp.where(ri == ci, 1.0, p)
    pk = p
    steps = c.bit_length() - 2
    for _ in range(steps):
        pk16 = pk.astype(BF16)
        pk = _mm(pk16, pk16)
        t = t + _mm(t.astype(BF16), pk.astype(BF16))
    sol = _mm(t.astype(BF16), rhs16)
    aw = _mm(attn16, sol.astype(BF16))
    u = sol[:, :dh]
    w16 = sol[:, dh:].astype(BF16)
    au = aw[:, :dh]
    qw16 = (q * eg - aw[:, dh:]).astype(BF16)
    ke16 = (k * jnp.exp(gtot - gcs)).astype(BF16)

    s = s_ref[d, j]
    chunks = range(r // c) if d == 0 else reversed(range(r // c))
    for cc in chunks:
        rows = slice(cc * c, (cc + 1) * c)
        ws = _mm(jnp.concatenate([w16[rows], qw16[rows]], axis=0), s.astype(BF16))
        v_new = u[rows] - ws[:c]
        o_ref[j, rows, :] = (ws[c:] + au[rows]).astype(o_ref.dtype)
        s = s * jnp.exp(gtot[cc * c:cc * c + 1, :]) + _tn(ke16[rows], v_new.astype(BF16))
    s_ref[d, j] = s


def _gdn_kernel(l_ref,
                qf_ref, kf_ref, vf_ref, c1f_ref, c2f_ref, c3f_ref, r2f_ref,
                qb_ref, kb_ref, vb_ref, c1b_ref, c2b_ref, c3b_ref, r2b_ref,
                of_ref, ob_ref, s_ref, *, hb):
    @pl.when(pl.program_id(1) == 0)
    def _():
        s_ref[...] = jnp.zeros_like(s_ref)

    for j in range(hb):
        head = pl.program_id(0) * hb + j
        _gdn_stream(qf_ref, kf_ref, vf_ref, c1f_ref, c2f_ref, c3f_ref, r2f_ref, s_ref, of_ref, j=j, d=0, head=head)
        _gdn_stream(qb_ref, kb_ref, vb_ref, c1b_ref, c2b_ref, c3b_ref, r2b_ref, s_ref, ob_ref, j=j, d=1, head=head)


def _gdn_call(layer, act, c1, c2, c3, r2, *, hb=2):
    seq = c1.shape[0]
    r = GDN_BLOCK
    nb = seq // r
    q0, k0, v0 = 24 // hb, 40 // hb, 56 // hb

    def dir_specs(rowblk):
        return [
            pl.BlockSpec((hb, r, LANES), lambda h, i, l: (q0 + h, rowblk(i), 0)),
            pl.BlockSpec((hb, r, LANES), lambda h, i, l: (k0 + h, rowblk(i), 0)),
            pl.BlockSpec((hb, r, LANES), lambda h, i, l: (v0 + h, rowblk(i), 0)),
            pl.BlockSpec((r, LANES), lambda h, i, l: (rowblk(i), 0)),
            pl.BlockSpec((r, LANES), lambda h, i, l: (rowblk(i), 0)),
            pl.BlockSpec((r, LANES), lambda h, i, l: (rowblk(i), 0)),
            pl.BlockSpec((LANES, r), lambda h, i, l: (0, rowblk(i))),
        ]

    fwd = lambda i: i
    bwd = lambda i: nb - 1 - i
    gs = pltpu.PrefetchScalarGridSpec(
        num_scalar_prefetch=1,
        grid=(GDN_HEADS // hb, nb),
        in_specs=dir_specs(fwd) + dir_specs(bwd),
        out_specs=[pl.BlockSpec((hb, r, LANES), lambda h, i, l: (h, fwd(i), 0)),
                   pl.BlockSpec((hb, r, LANES), lambda h, i, l: (h, bwd(i), 0))],
        scratch_shapes=[pltpu.VMEM((2, hb, GDN_HEAD_DIM, GDN_HEAD_DIM), F32)],
    )
    o_shape = jax.ShapeDtypeStruct((GDN_HEADS, seq, LANES), BF16)
    args = (act, act, act, c1, c2, c3, r2)
    return pl.pallas_call(
        functools.partial(_gdn_kernel, hb=hb),
        grid_spec=gs,
        out_shape=[o_shape, o_shape],
        compiler_params=_params(("parallel", "arbitrary")),
        name="gdn",
    )(layer, *args, *args)


def _outproj_kernel(l_ref, x_ref, yf_ref, yb_ref, xa_ref, zs_ref, zg_ref, of_ref, ob_ref,
                    dexp_ref, sg_ref, gg_ref, w_ref, post_ref, o_ref, ycat_ref, acc_ref):
    kstep = pl.program_id(1)
    gw = SSD_WIDTH // SSD_GROUPS

    @pl.when(kstep == 0)
    def _():
        for g in range(SSD_GROUPS):
            sl = slice(g * gw, (g + 1) * gw)
            xg = jnp.concatenate([xa_ref[4 * g + p] for p in range(4)], axis=1).astype(F32)
            y = yf_ref[:, sl].astype(F32) + yb_ref[:, sl].astype(F32) + dexp_ref[:, sl] * xg
            y = y * _silu(zs_ref[:, sl])
            ycat_ref[:, sl] = _rms(y, sg_ref[:, sl]).astype(BF16)
        for h in range(GDN_HEADS):
            sl = slice(h * LANES, (h + 1) * LANES)
            o = of_ref[h].astype(F32) + ob_ref[h].astype(F32)
            o = _rms(o, gg_ref[...]) * _silu(zg_ref[:, sl])
            ycat_ref[:, SSD_WIDTH + h * LANES:SSD_WIDTH + (h + 1) * LANES] = o.astype(BF16)
        acc_ref[...] = _mm(ycat_ref[:, :SSD_WIDTH], w_ref[...])

    @pl.when(kstep == 1)
    def _():
        m = acc_ref[...] + _mm(ycat_ref[:, SSD_WIDTH:], w_ref[...])
        o_ref[...] = x_ref[...] + _rms(m, post_ref[...])


def _outproj_call(layer, x, yf, yb, act, proj, of, ob, dexp, ssd_g, gdn_g, w_out, post_g, *, tm=256):
    seq, d = x.shape
    row_d = pl.BlockSpec((None, 1, d), lambda i, k, l: (l[0], 0, 0))
    gs = pltpu.PrefetchScalarGridSpec(
        num_scalar_prefetch=1,
        grid=(seq // tm, 2),
        in_specs=[
            pl.BlockSpec((tm, d), lambda i, k, l: (i, 0)),
            pl.BlockSpec((tm, SSD_WIDTH), lambda i, k, l: (i, 0)),
            pl.BlockSpec((tm, SSD_WIDTH), lambda i, k, l: (i, 0)),
            pl.BlockSpec((16, tm, LANES), lambda i, k, l: (0, i, 0)),
            pl.BlockSpec((tm, SSD_WIDTH), lambda i, k, l: (i, COL_ZS // SSD_WIDTH)),
            pl.BlockSpec((tm, GDN_WIDTH), lambda i, k, l: (i, COL_ZG // GDN_WIDTH)),
            pl.BlockSpec((GDN_HEADS, tm, LANES), lambda i, k, l: (0, i, 0)),
            pl.BlockSpec((GDN_HEADS, tm, LANES), lambda i, k, l: (0, i, 0)),
            row_d, row_d,
            pl.BlockSpec((None, 1, LANES), lambda i, k, l: (l[0], 0, 0)),
            pl.BlockSpec((None, d, d), lambda i, k, l: (l[0], k, 0)),
            row_d,
        ],
        out_specs=pl.BlockSpec((tm, d), lambda i, k, l: (i, 0)),
        scratch_shapes=[pltpu.VMEM((tm, 2 * d), BF16), pltpu.VMEM((tm, d), F32)],
    )
    return pl.pallas_call(
        _outproj_kernel,
        grid_spec=gs,
        out_shape=jax.ShapeDtypeStruct((seq, d), F32),
        compiler_params=_params(("parallel", "arbitrary")),
        name="outproj",
    )(layer, x, yf, yb, act, proj, proj, of, ob, dexp, ssd_g, gdn_g, w_out, post_g)


def kernel(x, ffn1_pre_g, ffn1_w_gu, ffn1_w_down, ffn1_post_g, mix_pre_g, w_in, ssd_conv_w, ssd_conv_b,
           ssd_dt_bias, ssd_a_log, ssd_d, ssd_norm_g, gdn_conv_w, gdn_dt_bias, gdn_a_log, gdn_norm_g, w_out,
           mix_post_g, ffn2_pre_g, ffn2_w_gu, ffn2_w_down, ffn2_post_g):
    row = lambda g: g.astype(F32).reshape(DEPTH, 1, -1)

    s0, s1, s2 = SSD_WIDTH, SSD_WIDTH + SSD_CONV_CH, SSD_WIDTH + SSD_CONV_CH + 2 * SSD_HEADS
    g1, g2 = s2 + GDN_CONV_CH, s2 + GDN_CONV_CH + GDN_WIDTH
    w_in_p = jnp.concatenate([w_in[..., :s0], w_in[..., g1:g2], w_in[..., s0:s1], w_in[..., s2:g1],
                              w_in[..., s1:s2], w_in[..., g2:]], axis=-1).astype(BF16)
    conv_w = jnp.concatenate([ssd_conv_w, gdn_conv_w], axis=-1).astype(F32)
    conv_b = jnp.concatenate([ssd_conv_b.astype(F32), jnp.zeros((DEPTH, GDN_CONV_CH), F32)], axis=-1)[:, None, :]
    pad = jnp.zeros((DEPTH, LANES - G_BETA), F32)
    bias = jnp.concatenate([ssd_dt_bias.reshape(DEPTH, -1), gdn_dt_bias.reshape(DEPTH, -1), pad], axis=-1)
    alog = jnp.concatenate([ssd_a_log.reshape(DEPTH, -1), gdn_a_log.reshape(DEPTH, -1), pad], axis=-1)
    brow, arow = bias[:, None, :], alog[:, None, :]
    bcol, acol = bias[:, :, None], alog[:, :, None]
    dexp = jnp.repeat(ssd_d.astype(F32), SSD_HEAD_DIM, axis=-1)[:, None, :]
    f1_gu, f1_dn = ffn1_w_gu.astype(BF16), ffn1_w_down.astype(BF16)
    f2_gu, f2_dn = ffn2_w_gu.astype(BF16), ffn2_w_down.astype(BF16)
    w_out16 = w_out.astype(BF16)
    f1_pre, f1_post, f2_pre, f2_post = row(ffn1_pre_g), row(ffn1_post_g), row(ffn2_pre_g), row(ffn2_post_g)
    m_pre, m_post, ssd_g, gdn_g = row(mix_pre_g), row(mix_post_g), row(ssd_norm_g), row(gdn_norm_g)

    def layer_fn(i, xc):
        layer = jnp.reshape(i, (1,)).astype(jnp.int32)
        xc = _ffn_call(layer, xc, f1_pre, f1_gu, f1_dn, f1_post)
        proj = _inproj_call(layer, xc, m_pre, w_in_p)
        gate_t = proj[:, COL_GATE:].T
        c1, c2, c3, r2 = _gate_call(layer, proj, gate_t, brow, arow, bcol, acol)
        act = _conv_call(layer, proj, conv_w, conv_b)
        yf, yb = _ssd_call(layer, act, c1, c2, c3, r2)
        of, ob = _gdn_call(layer, act, c1, c2, c3, r2)
        xc = _outproj_call(layer, xc, yf, yb, act, proj, of, ob, dexp, ssd_g, gdn_g, w_out16, m_post)
        xc = _ffn_call(layer, xc, f2_pre, f2_gu, f2_dn, f2_post)
        return xc

    out = lax.fori_loop(0, DEPTH, layer_fn, x.reshape(SEQ, D_MODEL).astype(F32))
    return out.reshape(x.shape)
```

```python
import functools

import jax
import jax.numpy as jnp
from jax import lax
from jax.experimental import pallas as pl
from jax.experimental.pallas import tpu as pltpu

F32 = jnp.float32
BF16 = jnp.bfloat16
HIGHEST = lax.Precision.HIGHEST

D_MODEL = 2048
SEQ = 8192
DEPTH = 4
EPS = 1e-6
D_FF = 5632
CONV_WIDTH = 5
SSD_HEADS = 32
SSD_HEAD_DIM = 64
SSD_WIDTH = 2048
SSD_GROUPS = 4
SSD_STATE = 128
SSD_CHUNK = 128
SSD_BC = 512
SSD_CONV_CH = SSD_WIDTH + 2 * SSD_BC
GDN_HEADS = 16
GDN_HEAD_DIM = 128
GDN_WIDTH = 2048
GDN_CHUNK = 64
GDN_CONV_CH = 3 * GDN_WIDTH
D_IN_PROJ = 13440

LANES = 128
COL_ZS = 0
COL_ZG = 2048
COL_CONV = 4096
N_CONV = SSD_CONV_CH + GDN_CONV_CH
COL_GATE = COL_CONV + N_CONV
N_CONV_BLK = N_CONV // LANES
G_SSD = 0
G_GDN = 64
G_BETA = 96

GDN_SOLVE_CHUNK = 128
GDN_INV_SHIFT = 6
GDN_BLOCK = 256
assert 2 << GDN_INV_SHIFT == GDN_SOLVE_CHUNK
assert GDN_SOLVE_CHUNK == SSD_CHUNK
VMEM_LIMIT = 56 * 1024 * 1024


def _mm(a, b, precision=None):
    return jnp.dot(a, b, preferred_element_type=F32, precision=precision)


def _nt(a, b, precision=None):
    return lax.dot_general(a, b, (((1,), (1,)), ((), ())), preferred_element_type=F32, precision=precision)


def _tn(a, b):
    return lax.dot_general(a, b, (((0,), (0,)), ((), ())), preferred_element_type=F32)


def _rms(x, g):
    return x * lax.rsqrt(jnp.mean(x * x, axis=-1, keepdims=True) + EPS) * g


def _silu(x):
    return x * jax.nn.sigmoid(x)


def _softplus(x):
    return jnp.maximum(x, 0.0) + jnp.log1p(jnp.exp(-jnp.abs(x)))


def _params(sem):
    return pltpu.CompilerParams(dimension_semantics=sem, vmem_limit_bytes=VMEM_LIMIT)


def _ffn_kernel(l_ref, x_ref, pre_ref, wg_ref, wu_ref, wd_ref, post_ref, o_ref, h_ref, acc_ref, *, nj):
    j = pl.program_id(1)

    @pl.when(j == 0)
    def _():
        h_ref[...] = _rms(x_ref[...], pre_ref[...]).astype(BF16)
        acc_ref[...] = jnp.zeros_like(acc_ref)

    h = h_ref[...]
    gate = _mm(h, wg_ref[...])
    up = _mm(h, wu_ref[...])
    act = (_silu(gate) * up).astype(BF16)
    acc_ref[...] += _mm(act, wd_ref[...])

    @pl.when(j == nj - 1)
    def _():
        o_ref[...] = x_ref[...] + 0.5 * _rms(acc_ref[...], post_ref[...])


def _ffn_call(layer, x, pre_g, w_gu, w_down, post_g, *, tm=512, tf=512):
    seq, d = x.shape
    nj = D_FF // tf
    gs = pltpu.PrefetchScalarGridSpec(
        num_scalar_prefetch=1,
        grid=(seq // tm, nj),
        in_specs=[
            pl.BlockSpec((tm, d), lambda i, j, l: (i, 0)),
            pl.BlockSpec((None, 1, d), lambda i, j, l: (l[0], 0, 0)),
            pl.BlockSpec((None, d, tf), lambda i, j, l: (l[0], 0, j)),
            pl.BlockSpec((None, d, tf), lambda i, j, l: (l[0], 0, j + nj)),
            pl.BlockSpec((None, tf, d), lambda i, j, l: (l[0], j, 0)),
            pl.BlockSpec((None, 1, d), lambda i, j, l: (l[0], 0, 0)),
        ],
        out_specs=pl.BlockSpec((tm, d), lambda i, j, l: (i, 0)),
        scratch_shapes=[pltpu.VMEM((tm, d), BF16), pltpu.VMEM((tm, d), F32)],
    )
    return pl.pallas_call(
        functools.partial(_ffn_kernel, nj=nj),
        grid_spec=gs,
        out_shape=jax.ShapeDtypeStruct((seq, d), F32),
        compiler_params=_params(("parallel", "arbitrary")),
        name="ffn",
    )(layer, x, pre_g, w_gu, w_gu, w_down, post_g)


HALO = 16
CONV_SUB = 256


def _inproj_kernel(l_ref, xp_ref, x_ref, xn_ref, pre_ref, w_ref, wg_ref, cw_ref, cb_ref,
                   z_ref, gate_ref, act_ref, h_ref, acc_ref, *, tm, ni, nz):
    i = pl.program_id(0)
    j = pl.program_id(1)

    @pl.when(j == 0)
    def _():
        pre = pre_ref[...]
        h_ref[0:HALO, :] = jnp.where(i > 0, _rms(xp_ref[...], pre), 0.0).astype(BF16)
        h_ref[HALO:HALO + tm, :] = _rms(x_ref[...], pre).astype(BF16)
        h_ref[HALO + tm:, :] = jnp.where(i < ni - 1, _rms(xn_ref[...], pre), 0.0).astype(BF16)
        gate_ref[...] = _mm(h_ref[HALO:HALO + tm, :], wg_ref[...])

    @pl.when(j < nz)
    def _():
        z_ref[...] = _mm(h_ref[HALO:HALO + tm, :], w_ref[...]).astype(BF16)

    @pl.when(j >= nz)
    def _():
        pad = CONV_WIDTH // 2
        h = h_ref[...]
        for s in range(w_ref.shape[1] // CONV_SUB):
            cols = slice(s * CONV_SUB, (s + 1) * CONV_SUB)
            acc_ref[s] = _mm(h, w_ref[:, cols])
            y = cb_ref[:, cols]
            for t in range(CONV_WIDTH):
                y = y + cw_ref[t:t + 1, cols] * acc_ref[s, HALO - pad + t:HALO - pad + t + tm, :]
            y = _silu(y).astype(BF16)
            for b in range(CONV_SUB // LANES):
                act_ref[s * (CONV_SUB // LANES) + b] = y[:, b * LANES:(b + 1) * LANES]


def _inproj_call(layer, x, pre_g, w_in, conv_w, conv_b, *, tm=512, tn=1024):
    seq, d = x.shape
    ni = seq // tm
    nz = COL_CONV // tn
    nj = COL_GATE // tn
    hb = tm // HALO
    gs = pltpu.PrefetchScalarGridSpec(
        num_scalar_prefetch=1,
        grid=(ni, nj),
        in_specs=[
            pl.BlockSpec((HALO, d), lambda i, j, l: (jnp.maximum(i * hb - 1, 0), 0)),
            pl.BlockSpec((tm, d), lambda i, j, l: (i, 0)),
            pl.BlockSpec((HALO, d), lambda i, j, l: (jnp.minimum((i + 1) * hb, seq // HALO - 1), 0)),
            pl.BlockSpec((None, 1, d), lambda i, j, l: (l[0], 0, 0)),
            pl.BlockSpec((None, d, tn), lambda i, j, l: (l[0], 0, j)),
            pl.BlockSpec((None, d, LANES), lambda i, j, l: (l[0], 0, COL_GATE // LANES)),
            pl.BlockSpec((None, CONV_WIDTH, tn), lambda i, j, l: (l[0], 0, jnp.maximum(j - nz, 0))),
            pl.BlockSpec((None, 1, tn), lambda i, j, l: (l[0], 0, jnp.maximum(j - nz, 0))),
        ],
        out_specs=[
            pl.BlockSpec((tm, tn), lambda i, j, l: (i, jnp.minimum(j, nz - 1))),
            pl.BlockSpec((tm, LANES), lambda i, j, l: (i, 0)),
            pl.BlockSpec((tn // LANES, tm, LANES), lambda i, j, l: (jnp.maximum(j - nz, 0), i, 0)),
        ],
        scratch_shapes=[pltpu.VMEM((tm + 2 * HALO, d), BF16),
                        pltpu.VMEM((tn // CONV_SUB, tm + 2 * HALO, CONV_SUB), F32)],
    )
    return pl.pallas_call(
        functools.partial(_inproj_kernel, tm=tm, ni=ni, nz=nz),
        grid_spec=gs,
        out_shape=[jax.ShapeDtypeStruct((seq, COL_CONV), BF16),
                   jax.ShapeDtypeStruct((seq, LANES), F32),
                   jax.ShapeDtypeStruct((N_CONV_BLK, seq, LANES), BF16)],
        compiler_params=_params(("parallel", "arbitrary")),
        name="inproj",
    )(layer, x, x, x, pre_g, w_in, w_in, conv_w, conv_b)


def _gate_kernel(l_ref, raw_ref, rawt_ref, brow_ref, arow_ref, bcol_ref, acol_ref,
                 c1_ref, c2_ref, c3_ref, r2_ref):
    q = SSD_CHUNK
    raw = raw_ref[...]
    lane = lax.broadcasted_iota(jnp.int32, (q, LANES), 1)
    sp = _softplus(raw + brow_ref[...])
    c1_ref[...] = jnp.where(lane < G_BETA, sp, jax.nn.sigmoid(raw))
    a = jnp.where(lane < G_BETA, sp * (-jnp.exp(arow_ref[...])), 0.0)

    ri = lax.broadcasted_iota(jnp.int32, (q, q), 0)
    ci = lax.broadcasted_iota(jnp.int32, (q, q), 1)
    lower = (ri >= ci).astype(F32)
    upper = (ri <= ci).astype(F32)
    ones = jnp.ones((q, q), F32)

    def is_fwd(idx):
        return (idx < G_SSD + SSD_HEADS) | ((idx >= G_GDN) & (idx < G_GDN + GDN_HEADS))

    c2_ref[...] = jnp.where(is_fwd(lane), _mm(lower, a, HIGHEST), _mm(upper, a, HIGHEST))
    c3_ref[...] = _mm(ones, a, HIGHEST)

    sub = lax.broadcasted_iota(jnp.int32, (LANES, q), 0)
    at = jnp.where(sub < G_BETA, _softplus(rawt_ref[...] + bcol_ref[...]) * (-jnp.exp(acol_ref[...])), 0.0)
    r2_ref[...] = jnp.where(is_fwd(sub), _nt(at, lower, HIGHEST), _nt(at, upper, HIGHEST))


def _gate_call(layer, gate, gate_t, brow, arow, bcol, acol):
    seq = gate.shape[0]
    q = SSD_CHUNK
    row_p = pl.BlockSpec((None, 1, LANES), lambda i, l: (l[0], 0, 0))
    col_p = pl.BlockSpec((None, LANES, 1), lambda i, l: (l[0], 0, 0))
    colform = pl.BlockSpec((q, LANES), lambda i, l: (i, 0))
    gs = pltpu.PrefetchScalarGridSpec(
        num_scalar_prefetch=1,
        grid=(seq // q,),
        in_specs=[
            colform,
            pl.BlockSpec((LANES, q), lambda i, l: (0, i)),
            row_p, row_p, col_p, col_p,
        ],
        out_specs=[colform, colform, colform, pl.BlockSpec((LANES, q), lambda i, l: (0, i))],
    )
    col_shape = jax.ShapeDtypeStruct((seq, LANES), F32)
    return pl.pallas_call(
        _gate_kernel,
        grid_spec=gs,
        out_shape=[col_shape, col_shape, col_shape, jax.ShapeDtypeStruct((LANES, seq), F32)],
        compiler_params=_params(("parallel",)),
        name="gateprep",
    )(layer, gate, gate_t, brow, arow, bcol, acol)


def _expand(x, e):
    hi = x.astype(BF16)
    lo = (x - hi.astype(F32)).astype(BF16)
    n = x.shape[0]
    r = _mm(jnp.concatenate([hi, lo], axis=0), e)
    return r[:n] + r[n:]


def _ssd_dir(x_ref, b_ref, c_ref, c1_ref, c2_ref, c3_ref, r2_ref, s_ref, y_ref, *, d):
    q = SSD_CHUNK
    gw = SSD_WIDTH // SSD_GROUPS
    hpg = SSD_HEADS // SSD_GROUPS
    dt_all = c1_ref[...]
    cs_all = c2_ref[...]
    tot_all = c3_ref[...]
    stacked = jnp.concatenate(
        [dt_all, jnp.exp(cs_all), jnp.exp(tot_all - cs_all), jnp.exp(tot_all[0:16])], axis=0)
    ri = lax.broadcasted_iota(jnp.int32, (q, q), 0)
    ci = lax.broadcasted_iota(jnp.int32, (q, q), 1)
    mask = (ri >= ci) if d == 0 else (ri <= ci)
    kk = lax.broadcasted_iota(jnp.int32, (LANES, gw), 0)
    nn = lax.broadcasted_iota(jnp.int32, (LANES, gw), 1)
    lane = lax.broadcasted_iota(jnp.int32, (q, LANES), 1)
    for g in range(SSD_GROUPS):
        base = G_SSD + SSD_HEADS * d + hpg * g
        sel = (kk == base + (nn >> 6)).astype(BF16)
        ex = _expand(stacked, sel)
        dt_e, ecs_e, dte_e, etot_e = ex[0:q], ex[q:2 * q], ex[2 * q:3 * q], ex[3 * q:3 * q + 1]
        xg = jnp.concatenate([x_ref[4 * g + p] for p in range(4)], axis=1).astype(F32)
        xdt = xg * dt_e
        xdt16 = xdt.astype(BF16)
        xdte16 = (xdt * dte_e).astype(BF16)
        bg = b_ref[g]
        cg = c_ref[g]
        cb = _nt(cg, bg)
        s_prev = s_ref[d, g]
        y_off = _mm(cg, s_prev.astype(BF16)) * ecs_e
        s_ref[d, g] = s_prev * etot_e + _tn(bg, xdte16)
        ys = []
        for pair in range(hpg // 2):
            ms = []
            for hh in (2 * pair, 2 * pair + 1):
                r = base + hh
                dec = jnp.exp(jnp.where(mask, cs_all[:, r:r + 1] - r2_ref[r:r + 1, :], -jnp.inf))
                ms.append((cb * dec).astype(BF16))
            lhs = jnp.concatenate(ms, axis=1)
            xp = xdt16[:, pair * LANES:(pair + 1) * LANES]
            zero = jnp.zeros_like(xp)
            rhs = jnp.concatenate([jnp.where(lane < SSD_HEAD_DIM, xp, zero),
                                   jnp.where(lane >= SSD_HEAD_DIM, xp, zero)], axis=0)
            ys.append(_mm(lhs, rhs))
        y = jnp.concatenate(ys, axis=1) + y_off
        y_ref[:, g * gw:(g + 1) * gw] = y.astype(BF16)


def _ssd_kernel(l_ref,
                xf_ref, bf_ref, cf_ref, c1f_ref, c2f_ref, c3f_ref, r2f_ref,
                xb_ref, bb_ref, cb_ref, c1b_ref, c2b_ref, c3b_ref, r2b_ref,
                yf_ref, yb_ref, s_ref):
    @pl.when(pl.program_id(0) == 0)
    def _():
        s_ref[...] = jnp.zeros_like(s_ref)

    _ssd_dir(xf_ref, bf_ref, cf_ref, c1f_ref, c2f_ref, c3f_ref, r2f_ref, s_ref, yf_ref, d=0)
    _ssd_dir(xb_ref, bb_ref, cb_ref, c1b_ref, c2b_ref, c3b_ref, r2b_ref, s_ref, yb_ref, d=1)


def _ssd_call(layer, act, c1, c2, c3, r2):
    seq = c1.shape[0]
    q = SSD_CHUNK
    nc = seq // q

    def dir_specs(rowblk):
        return [
            pl.BlockSpec((16, q, LANES), lambda i, l: (0, rowblk(i), 0)),
            pl.BlockSpec((4, q, LANES), lambda i, l: (4, rowblk(i), 0)),
            pl.BlockSpec((4, q, LANES), lambda i, l: (5, rowblk(i), 0)),
            pl.BlockSpec((q, LANES), lambda i, l: (rowblk(i), 0)),
            pl.BlockSpec((q, LANES), lambda i, l: (rowblk(i), 0)),
            pl.BlockSpec((q, LANES), lambda i, l: (rowblk(i), 0)),
            pl.BlockSpec((LANES, q), lambda i, l: (0, rowblk(i))),
        ]

    fwd = lambda i: i
    bwd = lambda i: nc - 1 - i
    gs = pltpu.PrefetchScalarGridSpec(
        num_scalar_prefetch=1,
        grid=(nc,),
        in_specs=dir_specs(fwd) + dir_specs(bwd),
        out_specs=[pl.BlockSpec((q, SSD_WIDTH), lambda i, l: (fwd(i), 0)),
                   pl.BlockSpec((q, SSD_WIDTH), lambda i, l: (bwd(i), 0))],
        scratch_shapes=[pltpu.VMEM((2, SSD_GROUPS, SSD_STATE, SSD_WIDTH // SSD_GROUPS), F32)],
    )
    y_shape = jax.ShapeDtypeStruct((seq, SSD_WIDTH), BF16)
    args = (act, act, act, c1, c2, c3, r2)
    return pl.pallas_call(
        _ssd_kernel,
        grid_spec=gs,
        out_shape=[y_shape, y_shape],
        compiler_params=_params(("arbitrary",)),
        name="ssd",
    )(layer, *args, *args)


def _gdn_prep(q_ref, k_ref, v_ref, c1_ref, c2_ref, c3_ref, r2_ref, *, j, d, head, rows):
    c = GDN_SOLVE_CHUNK
    dh = GDN_HEAD_DIM
    q = q_ref[j, rows, :].astype(F32)
    k = k_ref[j, rows, :].astype(F32)
    v = v_ref[j, rows, :].astype(F32)
    q = q * (lax.rsqrt(jnp.sum(q * q, axis=-1, keepdims=True) + EPS) * (dh ** -0.5))
    k = k * lax.rsqrt(jnp.sum(k * k, axis=-1, keepdims=True) + EPS)

    lane = lax.broadcasted_iota(jnp.int32, (c, LANES), 1)
    ig = G_GDN + GDN_HEADS * d + head
    ib = G_BETA + GDN_HEADS * d + head
    gcs = jnp.sum(jnp.where(lane == ig, c2_ref[rows, :], 0.0), axis=-1, keepdims=True)
    gtot = jnp.sum(jnp.where(lane == ig, c3_ref[rows, :], 0.0), axis=-1, keepdims=True)
    beta = jnp.sum(jnp.where(lane == ib, c1_ref[rows, :], 0.0), axis=-1, keepdims=True)
    grow = r2_ref[pl.ds(ig, 1), :][:, rows]

    ri = lax.broadcasted_iota(jnp.int32, (c, c), 0)
    ci = lax.broadcasted_iota(jnp.int32, (c, c), 1)
    incl = (ri >= ci) if d == 0 else (ri <= ci)
    strict = (ri > ci) if d == 0 else (ri < ci)
    decay = jnp.exp(jnp.where(incl, gcs - grow, -jnp.inf))

    kb = k * beta
    eg = jnp.exp(gcs)
    kq = _nt(jnp.concatenate([kb, q], axis=0).astype(BF16), k.astype(BF16))
    p = jnp.where(strict, -(kq[:c] * decay), 0.0)
    same = (ri >> GDN_INV_SHIFT) == (ci >> GDN_INV_SHIFT)
    p_diag = jnp.where(same, p, 0.0)
    eye = ri == ci
    return dict(
        eye=eye,
        p16=p_diag.astype(BF16),
        p_off16=jnp.where(same, 0.0, p).astype(BF16),
        t16=jnp.where(eye, 1.0, p_diag).astype(BF16),
        attn16=(kq[c:] * decay).astype(BF16),
        rhs16=jnp.concatenate([v * beta, kb * eg], axis=1).astype(BF16),
        qeg=q * eg,
        ket16=(k * jnp.exp(gtot - gcs)).T.astype(BF16),
        ld=jnp.exp(gtot[0:1, :]),
    )


def _gdn_kernel(l_ref,
                qf_ref, kf_ref, vf_ref, c1f_ref, c2f_ref, c3f_ref, r2f_ref,
                qb_ref, kb_ref, vb_ref, c1b_ref, c2b_ref, c3b_ref, r2b_ref,
                of_ref, ob_ref, s_ref, *, hb):
    @pl.when(pl.program_id(1) == 0)
    def _():
        s_ref[...] = jnp.zeros_like(s_ref)

    c = GDN_SOLVE_CHUNK
    dh = GDN_HEAD_DIM
    ncb = GDN_BLOCK // c
    in_refs = ((qf_ref, kf_ref, vf_ref, c1f_ref, c2f_ref, c3f_ref, r2f_ref),
               (qb_ref, kb_ref, vb_ref, c1b_ref, c2b_ref, c3b_ref, r2b_ref))
    out_refs = (of_ref, ob_ref)
    streams = [(j, d) for j in range(hb) for d in range(2)]
    state = {sd: s_ref[sd[1], sd[0]] for sd in streams}

    def scan_read(units):
        for (j, d), u in units.items():
            ws = _mm(u["lhs16"], state[(j, d)].astype(BF16))
            u["v_new16"] = (u["u"] - ws[:c]).astype(BF16)
            out_refs[d][j, u["rows"], :] = (ws[c:] + u["au"]).astype(BF16)

    def scan_write(units):
        for sd, u in units.items():
            state[sd] = state[sd] * u["ld"] + _mm(u["ket16"], u["v_new16"])

    prev = None
    for phase in range(ncb):
        units = {}
        for (j, d) in streams:
            cc = phase if d == 0 else ncb - 1 - phase
            rows = slice(cc * c, (cc + 1) * c)
            units[(j, d)] = _gdn_prep(*in_refs[d], j=j, d=d, head=pl.program_id(0) * hb + j, rows=rows)
            units[(j, d)]["rows"] = rows
        woven = [] if prev is None else [functools.partial(scan_read, prev), functools.partial(scan_write, prev)]
        for level in range(GDN_INV_SHIFT - 1):
            for u in units.values():
                u["p16"] = _mm(u["p16"], u["p16"]).astype(BF16)
            for u in units.values():
                u["t16"] = u["t16"] + _mm(u["t16"], u["p16"]).astype(BF16)
            if level < len(woven):
                woven[level]()
        for u in units.values():
            u["x16"] = _mm(u["p_off16"], u["t16"]).astype(BF16)
        for u in units.values():
            u["t16"] = u["t16"] + _mm(u["t16"], u["x16"]).astype(BF16)
        for u in units.values():
            u["sol"] = _mm(u["t16"], u["rhs16"])
        for u in units.values():
            aw = _mm(u["attn16"], u["sol"].astype(BF16))
            u["u"] = u["sol"][:, :dh]
            u["au"] = aw[:, :dh]
            u["lhs16"] = jnp.concatenate([u["sol"][:, dh:], u["qeg"] - aw[:, dh:]], axis=0).astype(BF16)
        prev = units
    scan_read(prev)
    scan_write(prev)
    for (j, d) in streams:
        s_ref[d, j] = state[(j, d)]


def _gdn_call(layer, act, c1, c2, c3, r2, *, hb=4):
    seq = c1.shape[0]
    r = GDN_BLOCK
    nb = seq // r
    q0, k0, v0 = 24 // hb, 40 // hb, 56 // hb

    def dir_specs(rowblk):
        return [
            pl.BlockSpec((hb, r, LANES), lambda h, i, l: (q0 + h, rowblk(i), 0)),
            pl.BlockSpec((hb, r, LANES), lambda h, i, l: (k0 + h, rowblk(i), 0)),
            pl.BlockSpec((hb, r, LANES), lambda h, i, l: (v0 + h, rowblk(i), 0)),
            pl.BlockSpec((r, LANES), lambda h, i, l: (rowblk(i), 0)),
            pl.BlockSpec((r, LANES), lambda h, i, l: (rowblk(i), 0)),
            pl.BlockSpec((r, LANES), lambda h, i, l: (rowblk(i), 0)),
            pl.BlockSpec((LANES, r), lambda h, i, l: (0, rowblk(i))),
        ]

    fwd = lambda i: i
    bwd = lambda i: nb - 1 - i
    gs = pltpu.PrefetchScalarGridSpec(
        num_scalar_prefetch=1,
        grid=(GDN_HEADS // hb, nb),
        in_specs=dir_specs(fwd) + dir_specs(bwd),
        out_specs=[pl.BlockSpec((hb, r, LANES), lambda h, i, l: (h, fwd(i), 0)),
                   pl.BlockSpec((hb, r, LANES), lambda h, i, l: (h, bwd(i), 0))],
        scratch_shapes=[pltpu.VMEM((2, hb, GDN_HEAD_DIM, GDN_HEAD_DIM), F32)],
    )
    o_shape = jax.ShapeDtypeStruct((GDN_HEADS, seq, LANES), BF16)
    args = (act, act, act, c1, c2, c3, r2)
    return pl.pallas_call(
        functools.partial(_gdn_kernel, hb=hb),
        grid_spec=gs,
        out_shape=[o_shape, o_shape],
        compiler_params=_params(("parallel", "arbitrary")),
        name="gdn",
    )(layer, *args, *args)


def _outproj_kernel(l_ref, x_ref, yf_ref, yb_ref, xa_ref, zs_ref, zg_ref, of_ref, ob_ref,
                    dexp_ref, sg_ref, gg_ref, w_ref, post_ref, o_ref):
    gw = SSD_WIDTH // SSD_GROUPS
    hq = gw // LANES
    acc = None
    for g in range(SSD_GROUPS):
        sl = slice(g * gw, (g + 1) * gw)
        xg = jnp.concatenate([xa_ref[hq * g + p] for p in range(hq)], axis=1).astype(F32)
        y = yf_ref[:, sl].astype(F32) + yb_ref[:, sl].astype(F32) + dexp_ref[:, sl] * xg
        y = y * _silu(zs_ref[:, sl].astype(F32))
        part = _mm(_rms(y, sg_ref[:, sl]).astype(BF16), w_ref[sl, :])
        acc = part if acc is None else acc + part
    for g in range(GDN_WIDTH // gw):
        os = []
        for h in range(hq * g, hq * (g + 1)):
            o = of_ref[h].astype(F32) + ob_ref[h].astype(F32)
            o = _rms(o, gg_ref[...]) * _silu(zg_ref[:, h * LANES:(h + 1) * LANES].astype(F32))
            os.append(o.astype(BF16))
        acc = acc + _mm(jnp.concatenate(os, axis=1), w_ref[SSD_WIDTH + g * gw:SSD_WIDTH + (g + 1) * gw, :])
    o_ref[...] = x_ref[...] + _rms(acc, post_ref[...])


def _outproj_call(layer, x, yf, yb, act, z, of, ob, dexp, ssd_g, gdn_g, w_out, post_g, *, tm=256):
    seq, d = x.shape
    row_d = pl.BlockSpec((None, 1, d), lambda i, l: (l[0], 0, 0))
    gs = pltpu.PrefetchScalarGridSpec(
        num_scalar_prefetch=1,
        grid=(seq // tm,),
        in_specs=[
            pl.BlockSpec((tm, d), lambda i, l: (i, 0)),
            pl.BlockSpec((tm, SSD_WIDTH), lambda i, l: (i, 0)),
            pl.BlockSpec((tm, SSD_WIDTH), lambda i, l: (i, 0)),
            pl.BlockSpec((16, tm, LANES), lambda i, l: (0, i, 0)),
            pl.BlockSpec((tm, SSD_WIDTH), lambda i, l: (i, COL_ZS // SSD_WIDTH)),
            pl.BlockSpec((tm, GDN_WIDTH), lambda i, l: (i, COL_ZG // GDN_WIDTH)),
            pl.BlockSpec((GDN_HEADS, tm, LANES), lambda i, l: (0, i, 0)),
            pl.BlockSpec((GDN_HEADS, tm, LANES), lambda i, l: (0, i, 0)),
            row_d, row_d,
            pl.BlockSpec((None, 1, LANES), lambda i, l: (l[0], 0, 0)),
            pl.BlockSpec((None, 2 * d, d), lambda i, l: (l[0], 0, 0), pipeline_mode=pl.Buffered(1)),
            row_d,
        ],
        out_specs=pl.BlockSpec((tm, d), lambda i, l: (i, 0)),
    )
    return pl.pallas_call(
        _outproj_kernel,
        grid_spec=gs,
        out_shape=jax.ShapeDtypeStruct((seq, d), F32),
        compiler_params=_params(("parallel",)),
        name="outproj",
    )(layer, x, yf, yb, act, z, z, of, ob, dexp, ssd_g, gdn_g, w_out, post_g)


def kernel(x, ffn1_pre_g, ffn1_w_gu, ffn1_w_down, ffn1_post_g, mix_pre_g, w_in, ssd_conv_w, ssd_conv_b,
           ssd_dt_bias, ssd_a_log, ssd_d, ssd_norm_g, gdn_conv_w, gdn_dt_bias, gdn_a_log, gdn_norm_g, w_out,
           mix_post_g, ffn2_pre_g, ffn2_w_gu, ffn2_w_down, ffn2_post_g):
    row = lambda g: g.astype(F32).reshape(DEPTH, 1, -1)

    s0, s1, s2 = SSD_WIDTH, SSD_WIDTH + SSD_CONV_CH, SSD_WIDTH + SSD_CONV_CH + 2 * SSD_HEADS
    g1, g2 = s2 + GDN_CONV_CH, s2 + GDN_CONV_CH + GDN_WIDTH
    w_in_p = jnp.concatenate([w_in[..., :s0], w_in[..., g1:g2], w_in[..., s0:s1], w_in[..., s2:g1],
                              w_in[..., s1:s2], w_in[..., g2:]], axis=-1).astype(BF16)
    conv_w = jnp.concatenate([ssd_conv_w, gdn_conv_w], axis=-1).astype(F32)
    conv_b = jnp.concatenate([ssd_conv_b.astype(F32), jnp.zeros((DEPTH, GDN_CONV_CH), F32)], axis=-1)[:, None, :]
    pad = jnp.zeros((DEPTH, LANES - G_BETA), F32)
    bias = jnp.concatenate([ssd_dt_bias.reshape(DEPTH, -1), gdn_dt_bias.reshape(DEPTH, -1), pad], axis=-1)
    alog = jnp.concatenate([ssd_a_log.reshape(DEPTH, -1), gdn_a_log.reshape(DEPTH, -1), pad], axis=-1)
    brow, arow = bias[:, None, :], alog[:, None, :]
    bcol, acol = bias[:, :, None], alog[:, :, None]
    dexp = jnp.repeat(ssd_d.astype(F32), SSD_HEAD_DIM, axis=-1)[:, None, :]
    f1_gu, f1_dn = ffn1_w_gu.astype(BF16), ffn1_w_down.astype(BF16)
    f2_gu, f2_dn = ffn2_w_gu.astype(BF16), ffn2_w_down.astype(BF16)
    w_out16 = w_out.astype(BF16)
    f1_pre, f1_post, f2_pre, f2_post = row(ffn1_pre_g), row(ffn1_post_g), row(ffn2_pre_g), row(ffn2_post_g)
    m_pre, m_post, ssd_g, gdn_g = row(mix_pre_g), row(mix_post_g), row(ssd_norm_g), row(gdn_norm_g)

    def layer_fn(i, xc):
        layer = jnp.reshape(i, (1,)).astype(jnp.int32)
        xc = _ffn_call(layer, xc, f1_pre, f1_gu, f1_dn, f1_post)
        z, gate, act = _inproj_call(layer, xc, m_pre, w_in_p, conv_w, conv_b)
        c1, c2, c3, r2 = _gate_call(layer, gate, gate.T, brow, arow, bcol, acol)
        yf, yb = _ssd_call(layer, act, c1, c2, c3, r2)
        of, ob = _gdn_call(layer, act, c1, c2, c3, r2)
        xc = _outproj_call(layer, xc, yf, yb, act, z, of, ob, dexp, ssd_g, gdn_g, w_out16, m_post)
        xc = _ffn_call(layer, xc, f2_pre, f2_gu, f2_dn, f2_post)
        return xc

    out = lax.fori_loop(0, DEPTH, layer_fn, x.reshape(SEQ, D_MODEL).astype(F32))
    return out.reshape(x.shape)
```

```python
import functools

import jax
import jax.numpy as jnp
from jax import lax
from jax.experimental import pallas as pl
from jax.experimental.pallas import tpu as pltpu

F32 = jnp.float32
BF16 = jnp.bfloat16
HIGHEST = lax.Precision.HIGHEST

D_MODEL = 2048
SEQ = 8192
DEPTH = 4
EPS = 1e-6
D_FF = 5632
CONV_WIDTH = 5
SSD_HEADS = 32
SSD_HEAD_DIM = 64
SSD_WIDTH = 2048
SSD_GROUPS = 4
SSD_STATE = 128
SSD_CHUNK = 128
SSD_BC = 512
SSD_CONV_CH = SSD_WIDTH + 2 * SSD_BC
GDN_HEADS = 16
GDN_HEAD_DIM = 128
GDN_WIDTH = 2048
GDN_CHUNK = 64
GDN_CONV_CH = 3 * GDN_WIDTH
D_IN_PROJ = 13440

LANES = 128
COL_ZS = 0
COL_ZG = 2048
COL_CONV = 4096
N_CONV = SSD_CONV_CH + GDN_CONV_CH
COL_GATE = COL_CONV + N_CONV
N_CONV_BLK = N_CONV // LANES
G_SSD = 0
G_GDN = 64
G_BETA = 96

GDN_SOLVE_CHUNK = 128
GDN_INV_SHIFT = 6
GDN_BLOCK = 256
assert 2 << GDN_INV_SHIFT == GDN_SOLVE_CHUNK
assert GDN_SOLVE_CHUNK == SSD_CHUNK
VMEM_LIMIT = 56 * 1024 * 1024


def _mm(a, b, precision=None):
    return jnp.dot(a, b, preferred_element_type=F32, precision=precision)


def _nt(a, b, precision=None):
    return lax.dot_general(a, b, (((1,), (1,)), ((), ())), preferred_element_type=F32, precision=precision)


def _tn(a, b):
    return lax.dot_general(a, b, (((0,), (0,)), ((), ())), preferred_element_type=F32)


def _rms(x, g):
    return x * lax.rsqrt(jnp.mean(x * x, axis=-1, keepdims=True) + EPS) * g


def _silu(x):
    return x * jax.nn.sigmoid(x)


def _softplus(x):
    return jnp.maximum(x, 0.0) + jnp.log1p(jnp.exp(-jnp.abs(x)))


def _params(sem):
    return pltpu.CompilerParams(dimension_semantics=sem, vmem_limit_bytes=VMEM_LIMIT)


def _ffn_kernel(l_ref, x_ref, pre_ref, wg_ref, wu_ref, wd_ref, post_ref, o_ref, h_ref, acc_ref, *, nj):
    j = pl.program_id(1)

    @pl.when(j == 0)
    def _():
        h_ref[...] = _rms(x_ref[...], pre_ref[...]).astype(BF16)
        acc_ref[...] = jnp.zeros_like(acc_ref)

    h = h_ref[...]
    gate = _mm(h, wg_ref[...])
    up = _mm(h, wu_ref[...])
    act = (_silu(gate) * up).astype(BF16)
    acc_ref[...] += _mm(act, wd_ref[...])

    @pl.when(j == nj - 1)
    def _():
        o_ref[...] = x_ref[...] + 0.5 * _rms(acc_ref[...], post_ref[...])


def _ffn_call(layer, x, pre_g, w_gu, w_down, post_g, *, tm=512, tf=512):
    seq, d = x.shape
    nj = D_FF // tf
    gs = pltpu.PrefetchScalarGridSpec(
        num_scalar_prefetch=1,
        grid=(seq // tm, nj),
        in_specs=[
            pl.BlockSpec((tm, d), lambda i, j, l: (i, 0)),
            pl.BlockSpec((None, 1, d), lambda i, j, l: (l[0], 0, 0)),
            pl.BlockSpec((None, d, tf), lambda i, j, l: (l[0], 0, j)),
            pl.BlockSpec((None, d, tf), lambda i, j, l: (l[0], 0, j + nj)),
            pl.BlockSpec((None, tf, d), lambda i, j, l: (l[0], j, 0)),
            pl.BlockSpec((None, 1, d), lambda i, j, l: (l[0], 0, 0)),
        ],
        out_specs=pl.BlockSpec((tm, d), lambda i, j, l: (i, 0)),
        scratch_shapes=[pltpu.VMEM((tm, d), BF16), pltpu.VMEM((tm, d), F32)],
    )
    return pl.pallas_call(
        functools.partial(_ffn_kernel, nj=nj),
        grid_spec=gs,
        out_shape=jax.ShapeDtypeStruct((seq, d), F32),
        compiler_params=_params(("parallel", "arbitrary")),
        name="ffn",
    )(layer, x, pre_g, w_gu, w_gu, w_down, post_g)


HALO = 16
CONV_SUB = 256


def _inproj_kernel(l_ref, xp_ref, x_ref, xn_ref, pre_ref, w_ref, wg_ref, cw_ref, cb_ref,
                   z_ref, gate_ref, act_ref, h_ref, *, tm, ni, nz):
    i = pl.program_id(0)
    j = pl.program_id(1)

    @pl.when(j == 0)
    def _():
        pre = pre_ref[...]
        h_ref[0:HALO, :] = jnp.where(i > 0, _rms(xp_ref[...], pre), 0.0).astype(BF16)
        h_ref[HALO:HALO + tm, :] = _rms(x_ref[...], pre).astype(BF16)
        h_ref[HALO + tm:, :] = jnp.where(i < ni - 1, _rms(xn_ref[...], pre), 0.0).astype(BF16)
        gate_ref[...] = _mm(h_ref[HALO:HALO + tm, :], wg_ref[...])

    @pl.when(j < nz)
    def _():
        z_ref[...] = _mm(h_ref[HALO:HALO + tm, :], w_ref[...]).astype(BF16)

    @pl.when(j >= nz)
    def _():
        pad = CONV_WIDTH // 2
        h = h_ref[...]
        for s in range(w_ref.shape[1] // CONV_SUB):
            cols = slice(s * CONV_SUB, (s + 1) * CONV_SUB)
            acc = _mm(h, w_ref[:, cols])
            y = cb_ref[:, cols]
            for t in range(CONV_WIDTH):
                y = y + cw_ref[t:t + 1, cols] * acc[HALO - pad + t:HALO - pad + t + tm, :]
            y = _silu(y).astype(BF16)
            for b in range(CONV_SUB // LANES):
                act_ref[s * (CONV_SUB // LANES) + b] = y[:, b * LANES:(b + 1) * LANES]


def _inproj_call(layer, x, pre_g, w_in, conv_w, conv_b, *, tm=512, tn=1024):
    seq, d = x.shape
    ni = seq // tm
    nz = COL_CONV // tn
    nj = COL_GATE // tn
    hb = tm // HALO
    gs = pltpu.PrefetchScalarGridSpec(
        num_scalar_prefetch=1,
        grid=(ni, nj),
        in_specs=[
            pl.BlockSpec((HALO, d), lambda i, j, l: (jnp.maximum(i * hb - 1, 0), 0)),
            pl.BlockSpec((tm, d), lambda i, j, l: (i, 0)),
            pl.BlockSpec((HALO, d), lambda i, j, l: (jnp.minimum((i + 1) * hb, seq // HALO - 1), 0)),
            pl.BlockSpec((None, 1, d), lambda i, j, l: (l[0], 0, 0)),
            pl.BlockSpec((None, d, tn), lambda i, j, l: (l[0], 0, j)),
            pl.BlockSpec((None, d, LANES), lambda i, j, l: (l[0], 0, COL_GATE // LANES)),
            pl.BlockSpec((None, CONV_WIDTH, tn), lambda i, j, l: (l[0], 0, jnp.maximum(j - nz, 0))),
            pl.BlockSpec((None, 1, tn), lambda i, j, l: (l[0], 0, jnp.maximum(j - nz, 0))),
        ],
        out_specs=[
            pl.BlockSpec((tm, tn), lambda i, j, l: (i, jnp.minimum(j, nz - 1))),
            pl.BlockSpec((tm, LANES), lambda i, j, l: (i, 0)),
            pl.BlockSpec((tn // LANES, tm, LANES), lambda i, j, l: (jnp.maximum(j - nz, 0), i, 0)),
        ],
        scratch_shapes=[pltpu.VMEM((tm + 2 * HALO, d), BF16)],
    )
    return pl.pallas_call(
        functools.partial(_inproj_kernel, tm=tm, ni=ni, nz=nz),
        grid_spec=gs,
        out_shape=[jax.ShapeDtypeStruct((seq, COL_CONV), BF16),
                   jax.ShapeDtypeStruct((seq, LANES), F32),
                   jax.ShapeDtypeStruct((N_CONV_BLK, seq, LANES), BF16)],
        compiler_params=_params(("parallel", "arbitrary")),
        name="inproj",
    )(layer, x, x, x, pre_g, w_in, w_in, conv_w, conv_b)


def _gate_kernel(l_ref, raw_ref, rawt_ref, brow_ref, arow_ref, bcol_ref, acol_ref,
                 c1_ref, c2_ref, c3_ref, r2_ref):
    q = SSD_CHUNK
    raw = raw_ref[...]
    lane = lax.broadcasted_iota(jnp.int32, (q, LANES), 1)
    sp = _softplus(raw + brow_ref[...])
    c1_ref[...] = jnp.where(lane < G_BETA, sp, jax.nn.sigmoid(raw))
    a = jnp.where(lane < G_BETA, sp * (-jnp.exp(arow_ref[...])), 0.0)

    ri = lax.broadcasted_iota(jnp.int32, (q, q), 0)
    ci = lax.broadcasted_iota(jnp.int32, (q, q), 1)
    lower = (ri >= ci).astype(F32)
    upper = (ri <= ci).astype(F32)
    ones = jnp.ones((q, q), F32)

    def is_fwd(idx):
        return (idx < G_SSD + SSD_HEADS) | ((idx >= G_GDN) & (idx < G_GDN + GDN_HEADS))

    c2_ref[...] = jnp.where(is_fwd(lane), _mm(lower, a, HIGHEST), _mm(upper, a, HIGHEST))
    c3_ref[...] = _mm(ones, a, HIGHEST)

    sub = lax.broadcasted_iota(jnp.int32, (LANES, q), 0)
    at = jnp.where(sub < G_BETA, _softplus(rawt_ref[...] + bcol_ref[...]) * (-jnp.exp(acol_ref[...])), 0.0)
    r2_ref[...] = jnp.where(is_fwd(sub), _nt(at, lower, HIGHEST), _nt(at, upper, HIGHEST))


def _gate_call(layer, gate, gate_t, brow, arow, bcol, acol):
    seq = gate.shape[0]
    q = SSD_CHUNK
    row_p = pl.BlockSpec((None, 1, LANES), lambda i, l: (l[0], 0, 0))
    col_p = pl.BlockSpec((None, LANES, 1), lambda i, l: (l[0], 0, 0))
    colform = pl.BlockSpec((q, LANES), lambda i, l: (i, 0))
    gs = pltpu.PrefetchScalarGridSpec(
        num_scalar_prefetch=1,
        grid=(seq // q,),
        in_specs=[
            colform,
            pl.BlockSpec((LANES, q), lambda i, l: (0, i)),
            row_p, row_p, col_p, col_p,
        ],
        out_specs=[colform, colform, colform, pl.BlockSpec((LANES, q), lambda i, l: (0, i))],
    )
    col_shape = jax.ShapeDtypeStruct((seq, LANES), F32)
    return pl.pallas_call(
        _gate_kernel,
        grid_spec=gs,
        out_shape=[col_shape, col_shape, col_shape, jax.ShapeDtypeStruct((LANES, seq), F32)],
        compiler_params=_params(("parallel",)),
        name="gateprep",
    )(layer, gate, gate_t, brow, arow, bcol, acol)


def _ssd_dir(x_ref, b_ref, c_ref, c1_ref, c2_ref, c3_ref, r2_ref, s_ref, y_ref, *, d):
    q = SSD_CHUNK
    gw = SSD_WIDTH // SSD_GROUPS
    hpg = SSD_HEADS // SSD_GROUPS
    dt_all = c1_ref[...]
    cs_all = c2_ref[...]
    tot_all = c3_ref[...]
    etot = jnp.exp(tot_all[0:16])
    etot_hi = etot.astype(BF16)
    etot_lo = (etot - etot_hi.astype(F32)).astype(BF16)
    stacked = jnp.concatenate([dt_all.astype(BF16), jnp.exp(cs_all).astype(BF16),
                               jnp.exp(tot_all - cs_all).astype(BF16), etot_hi, etot_lo], axis=0)
    ri = lax.broadcasted_iota(jnp.int32, (q, q), 0)
    ci = lax.broadcasted_iota(jnp.int32, (q, q), 1)
    mask = (ri >= ci) if d == 0 else (ri <= ci)
    kk = lax.broadcasted_iota(jnp.int32, (LANES, gw), 0)
    nn = lax.broadcasted_iota(jnp.int32, (LANES, gw), 1)
    lane = lax.broadcasted_iota(jnp.int32, (q, LANES), 1)
    for g in range(SSD_GROUPS):
        base = G_SSD + SSD_HEADS * d + hpg * g
        sel = (kk == base + (nn >> 6)).astype(BF16)
        ex = _mm(stacked, sel)
        dt_e, ecs_e, dte_e = ex[0:q], ex[q:2 * q], ex[2 * q:3 * q]
        etot_e = ex[3 * q:3 * q + 1] + ex[3 * q + 16:3 * q + 17]
        xg = jnp.concatenate([x_ref[4 * g + p] for p in range(4)], axis=1).astype(F32)
        xdt = xg * dt_e
        xdt16 = xdt.astype(BF16)
        xdte16 = (xdt * dte_e).astype(BF16)
        bg = b_ref[g]
        cg = c_ref[g]
        cb = _nt(cg, bg)
        s_prev = s_ref[d, g]
        y_off = _mm(cg, s_prev.astype(BF16)) * ecs_e
        s_ref[d, g] = s_prev * etot_e + _tn(bg, xdte16)
        ys = []
        for pair in range(hpg // 2):
            ms = []
            for hh in (2 * pair, 2 * pair + 1):
                r = base + hh
                dec = jnp.exp(jnp.where(mask, cs_all[:, r:r + 1] - r2_ref[r:r + 1, :], -jnp.inf))
                ms.append((cb * dec).astype(BF16))
            lhs = jnp.concatenate(ms, axis=1)
            xp = xdt16[:, pair * LANES:(pair + 1) * LANES]
            zero = jnp.zeros_like(xp)
            rhs = jnp.concatenate([jnp.where(lane < SSD_HEAD_DIM, xp, zero),
                                   jnp.where(lane >= SSD_HEAD_DIM, xp, zero)], axis=0)
            ys.append(_mm(lhs, rhs))
        y = jnp.concatenate(ys, axis=1) + y_off
        y_ref[:, g * gw:(g + 1) * gw] = y.astype(BF16)


def _ssd_kernel(l_ref,
                xf_ref, bf_ref, cf_ref, c1f_ref, c2f_ref, c3f_ref, r2f_ref,
                xb_ref, bb_ref, cb_ref, c1b_ref, c2b_ref, c3b_ref, r2b_ref,
                yf_ref, yb_ref, s_ref):
    @pl.when(pl.program_id(0) == 0)
    def _():
        s_ref[...] = jnp.zeros_like(s_ref)

    _ssd_dir(xf_ref, bf_ref, cf_ref, c1f_ref, c2f_ref, c3f_ref, r2f_ref, s_ref, yf_ref, d=0)
    _ssd_dir(xb_ref, bb_ref, cb_ref, c1b_ref, c2b_ref, c3b_ref, r2b_ref, s_ref, yb_ref, d=1)


def _ssd_call(layer, act, c1, c2, c3, r2):
    seq = c1.shape[0]
    q = SSD_CHUNK
    nc = seq // q

    def dir_specs(rowblk):
        return [
            pl.BlockSpec((16, q, LANES), lambda i, l: (0, rowblk(i), 0)),
            pl.BlockSpec((4, q, LANES), lambda i, l: (4, rowblk(i), 0)),
            pl.BlockSpec((4, q, LANES), lambda i, l: (5, rowblk(i), 0)),
            pl.BlockSpec((q, LANES), lambda i, l: (rowblk(i), 0)),
            pl.BlockSpec((q, LANES), lambda i, l: (rowblk(i), 0)),
            pl.BlockSpec((q, LANES), lambda i, l: (rowblk(i), 0)),
            pl.BlockSpec((LANES, q), lambda i, l: (0, rowblk(i))),
        ]

    fwd = lambda i: i
    bwd = lambda i: nc - 1 - i
    gs = pltpu.PrefetchScalarGridSpec(
        num_scalar_prefetch=1,
        grid=(nc,),
        in_specs=dir_specs(fwd) + dir_specs(bwd),
        out_specs=[pl.BlockSpec((q, SSD_WIDTH), lambda i, l: (fwd(i), 0)),
                   pl.BlockSpec((q, SSD_WIDTH), lambda i, l: (bwd(i), 0))],
        scratch_shapes=[pltpu.VMEM((2, SSD_GROUPS, SSD_STATE, SSD_WIDTH // SSD_GROUPS), F32)],
    )
    y_shape = jax.ShapeDtypeStruct((seq, SSD_WIDTH), BF16)
    args = (act, act, act, c1, c2, c3, r2)
    return pl.pallas_call(
        _ssd_kernel,
        grid_spec=gs,
        out_shape=[y_shape, y_shape],
        compiler_params=_params(("arbitrary",)),
        name="ssd",
    )(layer, *args, *args)


def _gdn_prep(q_ref, k_ref, v_ref, c1_ref, c2_ref, c3_ref, r2_ref, *, j, d, head, rows):
    c = GDN_SOLVE_CHUNK
    dh = GDN_HEAD_DIM
    q = q_ref[j, rows, :].astype(F32)
    k = k_ref[j, rows, :].astype(F32)
    v = v_ref[j, rows, :].astype(F32)
    q = q * (lax.rsqrt(jnp.sum(q * q, axis=-1, keepdims=True) + EPS) * (dh ** -0.5))
    k = k * lax.rsqrt(jnp.sum(k * k, axis=-1, keepdims=True) + EPS)

    lane = lax.broadcasted_iota(jnp.int32, (c, LANES), 1)
    ig = G_GDN + GDN_HEADS * d + head
    ib = G_BETA + GDN_HEADS * d + head
    gcs = jnp.sum(jnp.where(lane == ig, c2_ref[rows, :], 0.0), axis=-1, keepdims=True)
    gtot = jnp.sum(jnp.where(lane == ig, c3_ref[rows, :], 0.0), axis=-1, keepdims=True)
    beta = jnp.sum(jnp.where(lane == ib, c1_ref[rows, :], 0.0), axis=-1, keepdims=True)
    grow = r2_ref[pl.ds(ig, 1), :][:, rows]

    ri = lax.broadcasted_iota(jnp.int32, (c, c), 0)
    ci = lax.broadcasted_iota(jnp.int32, (c, c), 1)
    incl = (ri >= ci) if d == 0 else (ri <= ci)
    strict = (ri > ci) if d == 0 else (ri < ci)
    decay = jnp.exp(jnp.where(incl, gcs - grow, -jnp.inf))

    kb = k * beta
    eg = jnp.exp(gcs)
    kq = _nt(jnp.concatenate([kb, q], axis=0).astype(BF16), k.astype(BF16))
    p = jnp.where(strict, -(kq[:c] * decay), 0.0)
    half = c // 2
    left = lax.broadcasted_iota(jnp.int32, (half, LANES), 1) < half
    return dict(
        q=jnp.where(left, p[:half], p[half:]),
        poff16=jnp.where(left, p[half:], p[:half]).astype(BF16),
        attn16=(kq[c:] * decay).astype(BF16),
        rhs16=jnp.concatenate([v * beta, kb * eg], axis=1).astype(BF16),
        qeg=q * eg,
        ket16=(k * jnp.exp(gtot - gcs)).T.astype(BF16),
        ld=jnp.exp(gtot[0:1, :]),
    )


def _gdn_kernel(l_ref,
                qf_ref, kf_ref, vf_ref, c1f_ref, c2f_ref, c3f_ref, r2f_ref,
                qb_ref, kb_ref, vb_ref, c1b_ref, c2b_ref, c3b_ref, r2b_ref,
                of_ref, ob_ref, s_ref, *, hb):
    @pl.when(pl.program_id(1) == 0)
    def _():
        s_ref[...] = jnp.zeros_like(s_ref)

    c = GDN_SOLVE_CHUNK
    dh = GDN_HEAD_DIM
    ncb = GDN_BLOCK // c
    in_refs = ((qf_ref, kf_ref, vf_ref, c1f_ref, c2f_ref, c3f_ref, r2f_ref),
               (qb_ref, kb_ref, vb_ref, c1b_ref, c2b_ref, c3b_ref, r2b_ref))
    out_refs = (of_ref, ob_ref)
    streams = [(j, d) for j in range(hb) for d in range(2)]
    state = {sd: s_ref[sd[1], sd[0]] for sd in streams}

    half = c // 2
    lane_h = lax.broadcasted_iota(jnp.int32, (half, LANES), 1)
    row_h = lax.broadcasted_iota(jnp.int32, (half, LANES), 0)
    left = lane_h < half
    eye2 = (lane_h == row_h) | (lane_h == row_h + half)

    def blockdiag16(x):
        return jnp.concatenate([jnp.where(left, x, 0.0), jnp.where(left, 0.0, x)], axis=0).astype(BF16)

    def scan_read(units):
        for (j, d), u in units.items():
            ws = _mm(u["lhs16"], state[(j, d)].astype(BF16))
            u["v_new16"] = (u["u"] - ws[:c]).astype(BF16)
            out_refs[d][j, u["rows"], :] = (ws[c:] + u["au"]).astype(BF16)

    def scan_write(units):
        for sd, u in units.items():
            state[sd] = state[sd] * u["ld"] + _mm(u["ket16"], u["v_new16"])

    prev = None
    for phase in range(ncb):
        units = {}
        for (j, d) in streams:
            cc = phase if d == 0 else ncb - 1 - phase
            rows = slice(cc * c, (cc + 1) * c)
            units[(j, d)] = _gdn_prep(*in_refs[d], j=j, d=d, head=pl.program_id(0) * hb + j, rows=rows)
            units[(j, d)]["rows"] = rows
        woven = [] if prev is None else [functools.partial(scan_read, prev), functools.partial(scan_write, prev)]
        for u in units.values():
            u["t"] = jnp.where(eye2, 1.0, u["q"])
            u["q"] = _mm(u["q"].astype(BF16), blockdiag16(u["q"]))
        if woven:
            woven[0]()
        for level in range(1, GDN_INV_SHIFT):
            for u in units.values():
                rhs = blockdiag16(u["q"])
                if level == GDN_INV_SHIFT - 1:
                    u["t"] = u["t"] + _mm(u["t"].astype(BF16), rhs)
                else:
                    qt = _mm(jnp.concatenate([u["q"], u["t"]], axis=0).astype(BF16), rhs)
                    u["q"] = qt[:half]
                    u["t"] = u["t"] + qt[half:]
            if level < len(woven):
                woven[level]()
        for u in units.values():
            u["tbd16"] = blockdiag16(u["t"])
            u["y16"] = _mm(u["poff16"], u["tbd16"]).astype(BF16)
        zeros = jnp.zeros((half, LANES), BF16)
        for (j, d), u in units.items():
            tbd16 = u["tbd16"]
            if d == 0:
                z = _mm(tbd16[half:], jnp.concatenate([zeros, u["y16"]], axis=0))
                u["t16"] = jnp.concatenate([tbd16[:half], jnp.where(left, z, u["t"]).astype(BF16)], axis=0)
            else:
                z = _mm(tbd16[:half], jnp.concatenate([u["y16"], zeros], axis=0))
                u["t16"] = jnp.concatenate([jnp.where(left, u["t"], z).astype(BF16), tbd16[half:]], axis=0)
        for u in units.values():
            u["sol"] = _mm(u["t16"], u["rhs16"])
        for u in units.values():
            aw = _mm(u["attn16"], u["sol"].astype(BF16))
            u["u"] = u["sol"][:, :dh]
            u["au"] = aw[:, :dh]
            u["lhs16"] = jnp.concatenate([u["sol"][:, dh:], u["qeg"] - aw[:, dh:]], axis=0).astype(BF16)
        prev = units
    scan_read(prev)
    scan_write(prev)
    for (j, d) in streams:
        s_ref[d, j] = state[(j, d)]


def _gdn_call(layer, act, c1, c2, c3, r2, *, hb=8):
    seq = c1.shape[0]
    r = GDN_BLOCK
    nb = seq // r
    q0, k0, v0 = 24 // hb, 40 // hb, 56 // hb

    def dir_specs(rowblk):
        return [
            pl.BlockSpec((hb, r, LANES), lambda h, i, l: (q0 + h, rowblk(i), 0)),
            pl.BlockSpec((hb, r, LANES), lambda h, i, l: (k0 + h, rowblk(i), 0)),
            pl.BlockSpec((hb, r, LANES), lambda h, i, l: (v0 + h, rowblk(i), 0)),
            pl.BlockSpec((r, LANES), lambda h, i, l: (rowblk(i), 0)),
            pl.BlockSpec((r, LANES), lambda h, i, l: (rowblk(i), 0)),
            pl.BlockSpec((r, LANES), lambda h, i, l: (rowblk(i), 0)),
            pl.BlockSpec((LANES, r), lambda h, i, l: (0, rowblk(i))),
        ]

    fwd = lambda i: i
    bwd = lambda i: nb - 1 - i
    gs = pltpu.PrefetchScalarGridSpec(
        num_scalar_prefetch=1,
        grid=(GDN_HEADS // hb, nb),
        in_specs=dir_specs(fwd) + dir_specs(bwd),
        out_specs=[pl.BlockSpec((hb, r, LANES), lambda h, i, l: (h, fwd(i), 0)),
                   pl.BlockSpec((hb, r, LANES), lambda h, i, l: (h, bwd(i), 0))],
        scratch_shapes=[pltpu.VMEM((2, hb, GDN_HEAD_DIM, GDN_HEAD_DIM), F32)],
    )
    o_shape = jax.ShapeDtypeStruct((GDN_HEADS, seq, LANES), BF16)
    args = (act, act, act, c1, c2, c3, r2)
    return pl.pallas_call(
        functools.partial(_gdn_kernel, hb=hb),
        grid_spec=gs,
        out_shape=[o_shape, o_shape],
        compiler_params=_params(("parallel", "arbitrary")),
        name="gdn",
    )(layer, *args, *args)


def _outproj_kernel(l_ref, x_ref, yf_ref, yb_ref, xa_ref, zs_ref, zg_ref, of_ref, ob_ref,
                    dexp_ref, sg_ref, gg_ref, w_ref, post_ref, o_ref):
    gw = SSD_WIDTH // SSD_GROUPS
    hq = gw // LANES
    acc = None
    for g in range(SSD_GROUPS):
        sl = slice(g * gw, (g + 1) * gw)
        xg = jnp.concatenate([xa_ref[hq * g + p] for p in range(hq)], axis=1).astype(F32)
        y = yf_ref[:, sl].astype(F32) + yb_ref[:, sl].astype(F32) + dexp_ref[:, sl] * xg
        y = y * _silu(zs_ref[:, sl].astype(F32))
        part = _mm(_rms(y, sg_ref[:, sl]).astype(BF16), w_ref[sl, :])
        acc = part if acc is None else acc + part
    for g in range(GDN_WIDTH // gw):
        os = []
        for h in range(hq * g, hq * (g + 1)):
            o = of_ref[h].astype(F32) + ob_ref[h].astype(F32)
            o = _rms(o, gg_ref[...]) * _silu(zg_ref[:, h * LANES:(h + 1) * LANES].astype(F32))
            os.append(o.astype(BF16))
        acc = acc + _mm(jnp.concatenate(os, axis=1), w_ref[SSD_WIDTH + g * gw:SSD_WIDTH + (g + 1) * gw, :])
    o_ref[...] = x_ref[...] + _rms(acc, post_ref[...])


def _outproj_call(layer, x, yf, yb, act, z, of, ob, dexp, ssd_g, gdn_g, w_out, post_g, *, tm=256):
    seq, d = x.shape
    row_d = pl.BlockSpec((None, 1, d), lambda i, l: (l[0], 0, 0))
    gs = pltpu.PrefetchScalarGridSpec(
        num_scalar_prefetch=1,
        grid=(seq // tm,),
        in_specs=[
            pl.BlockSpec((tm, d), lambda i, l: (i, 0)),
            pl.BlockSpec((tm, SSD_WIDTH), lambda i, l: (i, 0)),
            pl.BlockSpec((tm, SSD_WIDTH), lambda i, l: (i, 0)),
            pl.BlockSpec((16, tm, LANES), lambda i, l: (0, i, 0)),
            pl.BlockSpec((tm, SSD_WIDTH), lambda i, l: (i, COL_ZS // SSD_WIDTH)),
            pl.BlockSpec((tm, GDN_WIDTH), lambda i, l: (i, COL_ZG // GDN_WIDTH)),
            pl.BlockSpec((GDN_HEADS, tm, LANES), lambda i, l: (0, i, 0)),
            pl.BlockSpec((GDN_HEADS, tm, LANES), lambda i, l: (0, i, 0)),
            row_d, row_d,
            pl.BlockSpec((None, 1, LANES), lambda i, l: (l[0], 0, 0)),
            pl.BlockSpec((None, 2 * d, d), lambda i, l: (l[0], 0, 0), pipeline_mode=pl.Buffered(1)),
            row_d,
        ],
        out_specs=pl.BlockSpec((tm, d), lambda i, l: (i, 0)),
    )
    return pl.pallas_call(
        _outproj_kernel,
        grid_spec=gs,
        out_shape=jax.ShapeDtypeStruct((seq, d), F32),
        compiler_params=_params(("parallel",)),
        name="outproj",
    )(layer, x, yf, yb, act, z, z, of, ob, dexp, ssd_g, gdn_g, w_out, post_g)


def kernel(x, ffn1_pre_g, ffn1_w_gu, ffn1_w_down, ffn1_post_g, mix_pre_g, w_in, ssd_conv_w, ssd_conv_b,
           ssd_dt_bias, ssd_a_log, ssd_d, ssd_norm_g, gdn_conv_w, gdn_dt_bias, gdn_a_log, gdn_norm_g, w_out,
           mix_post_g, ffn2_pre_g, ffn2_w_gu, ffn2_w_down, ffn2_post_g):
    row = lambda g: g.astype(F32).reshape(DEPTH, 1, -1)

    s0, s1, s2 = SSD_WIDTH, SSD_WIDTH + SSD_CONV_CH, SSD_WIDTH + SSD_CONV_CH + 2 * SSD_HEADS
    g1, g2 = s2 + GDN_CONV_CH, s2 + GDN_CONV_CH + GDN_WIDTH
    w_in_p = jnp.concatenate([w_in[..., a:b].astype(BF16)
                              for a, b in ((0, s0), (g1, g2), (s0, s1), (s2, g1), (s1, s2), (g2, D_IN_PROJ))], axis=-1)
    conv_w = jnp.concatenate([ssd_conv_w, gdn_conv_w], axis=-1).astype(F32)
    conv_b = jnp.concatenate([ssd_conv_b.astype(F32), jnp.zeros((DEPTH, GDN_CONV_CH), F32)], axis=-1)[:, None, :]
    pad = jnp.zeros((DEPTH, LANES - G_BETA), F32)
    bias = jnp.concatenate([ssd_dt_bias.reshape(DEPTH, -1), gdn_dt_bias.reshape(DEPTH, -1), pad], axis=-1)
    alog = jnp.concatenate([ssd_a_log.reshape(DEPTH, -1), gdn_a_log.reshape(DEPTH, -1), pad], axis=-1)
    brow, arow = bias[:, None, :], alog[:, None, :]
    bcol, acol = bias[:, :, None], alog[:, :, None]
    dexp = jnp.repeat(ssd_d.astype(F32), SSD_HEAD_DIM, axis=-1)[:, None, :]
    f1_gu, f1_dn = ffn1_w_gu.astype(BF16), ffn1_w_down.astype(BF16)
    f2_gu, f2_dn = ffn2_w_gu.astype(BF16), ffn2_w_down.astype(BF16)
    w_out16 = w_out.astype(BF16)
    f1_pre, f1_post, f2_pre, f2_post = row(ffn1_pre_g), row(ffn1_post_g), row(ffn2_pre_g), row(ffn2_post_g)
    m_pre, m_post, ssd_g, gdn_g = row(mix_pre_g), row(mix_post_g), row(ssd_norm_g), row(gdn_norm_g)

    def layer_fn(i, xc):
        layer = jnp.reshape(i, (1,)).astype(jnp.int32)
        xc = _ffn_call(layer, xc, f1_pre, f1_gu, f1_dn, f1_post)
        z, gate, act = _inproj_call(layer, xc, m_pre, w_in_p, conv_w, conv_b)
        c1, c2, c3, r2 = _gate_call(layer, gate, gate.T, brow, arow, bcol, acol)
        yf, yb = _ssd_call(layer, act, c1, c2, c3, r2)
        of, ob = _gdn_call(layer, act, c1, c2, c3, r2)
        xc = _outproj_call(layer, xc, yf, yb, act, z, of, ob, dexp, ssd_g, gdn_g, w_out16, m_post)
        xc = _ffn_call(layer, xc, f2_pre, f2_gu, f2_dn, f2_post)
        return xc

    out = lax.fori_loop(0, DEPTH, layer_fn, x.reshape(SEQ, D_MODEL).astype(F32))
    return out.reshape(x.shape)
```

```python
import functools

import jax
import jax.numpy as jnp
from jax import lax
from jax.experimental import pallas as pl
from jax.experimental.pallas import tpu as pltpu

F32 = jnp.float32
BF16 = jnp.bfloat16
HIGHEST = lax.Precision.HIGHEST

D_MODEL = 2048
SEQ = 8192
DEPTH = 4
EPS = 1e-6
D_FF = 5632
CONV_WIDTH = 5
SSD_HEADS = 32
SSD_HEAD_DIM = 64
SSD_WIDTH = 2048
SSD_GROUPS = 4
SSD_STATE = 128
SSD_CHUNK = 128
SSD_BC = 512
SSD_CONV_CH = SSD_WIDTH + 2 * SSD_BC
GDN_HEADS = 16
GDN_HEAD_DIM = 128
GDN_WIDTH = 2048
GDN_CHUNK = 64
GDN_CONV_CH = 3 * GDN_WIDTH
D_IN_PROJ = 13440

LANES = 128
COL_ZS = 0
COL_ZG = 2048
COL_CONV = 4096
N_CONV = SSD_CONV_CH + GDN_CONV_CH
COL_GATE = COL_CONV + N_CONV
N_CONV_BLK = N_CONV // LANES
G_SSD = 0
G_GDN = 64
G_BETA = 96

GDN_SOLVE_CHUNK = 128
GDN_INV_SHIFT = 6
GDN_BLOCK = 256
assert 2 << GDN_INV_SHIFT == GDN_SOLVE_CHUNK
assert GDN_SOLVE_CHUNK == SSD_CHUNK
VMEM_LIMIT = 56 * 1024 * 1024


def _mm(a, b, precision=None):
    return jnp.dot(a, b, preferred_element_type=F32, precision=precision)


def _nt(a, b, precision=None):
    return lax.dot_general(a, b, (((1,), (1,)), ((), ())), preferred_element_type=F32, precision=precision)


def _tn(a, b):
    return lax.dot_general(a, b, (((0,), (0,)), ((), ())), preferred_element_type=F32)


def _rms(x, g):
    return x * lax.rsqrt(jnp.mean(x * x, axis=-1, keepdims=True) + EPS) * g


def _silu(x):
    return x * jax.nn.sigmoid(x)


def _softplus(x):
    return jnp.maximum(x, 0.0) + jnp.log1p(jnp.exp(-jnp.abs(x)))


def _params(sem):
    return pltpu.CompilerParams(dimension_semantics=sem, vmem_limit_bytes=VMEM_LIMIT)


def _ffn_kernel(l_ref, x_ref, pre_ref, wg_ref, wu_ref, wd_ref, post_ref, o_ref, h_ref, acc_ref, *, nj):
    j = pl.program_id(1)

    @pl.when(j == 0)
    def _():
        h_ref[...] = _rms(x_ref[...], pre_ref[...]).astype(BF16)
        acc_ref[...] = jnp.zeros_like(acc_ref)

    h = h_ref[...]
    gate = _mm(h, wg_ref[...])
    up = _mm(h, wu_ref[...])
    act = (_silu(gate) * up).astype(BF16)
    acc_ref[...] += _mm(act, wd_ref[...])

    @pl.when(j == nj - 1)
    def _():
        o_ref[...] = x_ref[...] + 0.5 * _rms(acc_ref[...], post_ref[...])


def _ffn_call(layer, x, pre_g, w_gu, w_down, post_g, *, tm=512, tf=512):
    seq, d = x.shape
    nj = D_FF // tf
    gs = pltpu.PrefetchScalarGridSpec(
        num_scalar_prefetch=1,
        grid=(seq // tm, nj),
        in_specs=[
            pl.BlockSpec((tm, d), lambda i, j, l: (i, 0)),
            pl.BlockSpec((None, 1, d), lambda i, j, l: (l[0], 0, 0)),
            pl.BlockSpec((None, d, tf), lambda i, j, l: (l[0], 0, j)),
            pl.BlockSpec((None, d, tf), lambda i, j, l: (l[0], 0, j + nj)),
            pl.BlockSpec((None, tf, d), lambda i, j, l: (l[0], j, 0)),
            pl.BlockSpec((None, 1, d), lambda i, j, l: (l[0], 0, 0)),
        ],
        out_specs=pl.BlockSpec((tm, d), lambda i, j, l: (i, 0)),
        scratch_shapes=[pltpu.VMEM((tm, d), BF16), pltpu.VMEM((tm, d), F32)],
    )
    return pl.pallas_call(
        functools.partial(_ffn_kernel, nj=nj),
        grid_spec=gs,
        out_shape=jax.ShapeDtypeStruct((seq, d), F32),
        compiler_params=_params(("parallel", "arbitrary")),
        name="ffn",
    )(layer, x, pre_g, w_gu, w_gu, w_down, post_g)


HALO = 16
CONV_SUB = 256


def _inproj_kernel(l_ref, xp_ref, x_ref, xn_ref, pre_ref, w_ref, wg_ref, cw_ref, cb_ref,
                   z_ref, gate_ref, act_ref, h_ref, *, tm, ni, nz):
    i = pl.program_id(0)
    j = pl.program_id(1)

    @pl.when(j == 0)
    def _():
        pre = pre_ref[...]
        h_ref[0:HALO, :] = jnp.where(i > 0, _rms(xp_ref[...], pre), 0.0).astype(BF16)
        h_ref[HALO:HALO + tm, :] = _rms(x_ref[...], pre).astype(BF16)
        h_ref[HALO + tm:, :] = jnp.where(i < ni - 1, _rms(xn_ref[...], pre), 0.0).astype(BF16)
        gate_ref[...] = _mm(h_ref[HALO:HALO + tm, :], wg_ref[...])

    @pl.when(j < nz)
    def _():
        z_ref[...] = _mm(h_ref[HALO:HALO + tm, :], w_ref[...]).astype(BF16)

    @pl.when(j >= nz)
    def _():
        pad = CONV_WIDTH // 2
        h = h_ref[...]
        for s in range(w_ref.shape[1] // CONV_SUB):
            cols = slice(s * CONV_SUB, (s + 1) * CONV_SUB)
            acc = _mm(h, w_ref[:, cols])
            y = cb_ref[:, cols]
            for t in range(CONV_WIDTH):
                y = y + cw_ref[t:t + 1, cols] * acc[HALO - pad + t:HALO - pad + t + tm, :]
            y = _silu(y).astype(BF16)
            for b in range(CONV_SUB // LANES):
                act_ref[s * (CONV_SUB // LANES) + b] = y[:, b * LANES:(b + 1) * LANES]


def _inproj_call(layer, x, pre_g, w_in, conv_w, conv_b, *, tm=1024, tn=1024):
    seq, d = x.shape
    ni = seq // tm
    nz = COL_CONV // tn
    nj = COL_GATE // tn
    hb = tm // HALO
    gs = pltpu.PrefetchScalarGridSpec(
        num_scalar_prefetch=1,
        grid=(ni, nj),
        in_specs=[
            pl.BlockSpec((HALO, d), lambda i, j, l: (jnp.maximum(i * hb - 1, 0), 0)),
            pl.BlockSpec((tm, d), lambda i, j, l: (i, 0)),
            pl.BlockSpec((HALO, d), lambda i, j, l: (jnp.minimum((i + 1) * hb, seq // HALO - 1), 0)),
            pl.BlockSpec((None, 1, d), lambda i, j, l: (l[0], 0, 0)),
            pl.BlockSpec((None, d, tn), lambda i, j, l: (l[0], 0, j)),
            pl.BlockSpec((None, d, LANES), lambda i, j, l: (l[0], 0, COL_GATE // LANES)),
            pl.BlockSpec((None, CONV_WIDTH, tn), lambda i, j, l: (l[0], 0, jnp.maximum(j - nz, 0))),
            pl.BlockSpec((None, 1, tn), lambda i, j, l: (l[0], 0, jnp.maximum(j - nz, 0))),
        ],
        out_specs=[
            pl.BlockSpec((tm, tn), lambda i, j, l: (i, jnp.minimum(j, nz - 1))),
            pl.BlockSpec((tm, LANES), lambda i, j, l: (i, 0)),
            pl.BlockSpec((tn // LANES, tm, LANES), lambda i, j, l: (jnp.maximum(j - nz, 0), i, 0)),
        ],
        scratch_shapes=[pltpu.VMEM((tm + 2 * HALO, d), BF16)],
    )
    return pl.pallas_call(
        functools.partial(_inproj_kernel, tm=tm, ni=ni, nz=nz),
        grid_spec=gs,
        out_shape=[jax.ShapeDtypeStruct((seq, COL_CONV), BF16),
                   jax.ShapeDtypeStruct((seq, LANES), F32),
                   jax.ShapeDtypeStruct((N_CONV_BLK, seq, LANES), BF16)],
        compiler_params=_params(("parallel", "arbitrary")),
        name="inproj",
    )(layer, x, x, x, pre_g, w_in, w_in, conv_w, conv_b)


def _gate_kernel(l_ref, raw_ref, rawt_ref, brow_ref, arow_ref, bcol_ref, acol_ref,
                 c1_ref, c2_ref, c3_ref, r2_ref):
    q = SSD_CHUNK
    raw = raw_ref[...]
    lane = lax.broadcasted_iota(jnp.int32, (q, LANES), 1)
    sp = _softplus(raw + brow_ref[...])
    c1_ref[...] = jnp.where(lane < G_BETA, sp, jax.nn.sigmoid(raw))
    a = jnp.where(lane < G_BETA, sp * (-jnp.exp(arow_ref[...])), 0.0)

    ri = lax.broadcasted_iota(jnp.int32, (q, q), 0)
    ci = lax.broadcasted_iota(jnp.int32, (q, q), 1)
    lower = (ri >= ci).astype(F32)
    upper = (ri <= ci).astype(F32)
    ones = jnp.ones((q, q), F32)

    def is_fwd(idx):
        return (idx < G_SSD + SSD_HEADS) | ((idx >= G_GDN) & (idx < G_GDN + GDN_HEADS))

    c2_ref[...] = jnp.where(is_fwd(lane), _mm(lower, a, HIGHEST), _mm(upper, a, HIGHEST))
    c3_ref[...] = _mm(ones, a, HIGHEST)

    sub = lax.broadcasted_iota(jnp.int32, (LANES, q), 0)
    at = jnp.where(sub < G_BETA, _softplus(rawt_ref[...] + bcol_ref[...]) * (-jnp.exp(acol_ref[...])), 0.0)
    r2_ref[...] = jnp.where(is_fwd(sub), _nt(at, lower, HIGHEST), _nt(at, upper, HIGHEST))


def _gate_call(layer, gate, gate_t, brow, arow, bcol, acol):
    seq = gate.shape[0]
    q = SSD_CHUNK
    row_p = pl.BlockSpec((None, 1, LANES), lambda i, l: (l[0], 0, 0))
    col_p = pl.BlockSpec((None, LANES, 1), lambda i, l: (l[0], 0, 0))
    colform = pl.BlockSpec((q, LANES), lambda i, l: (i, 0))
    gs = pltpu.PrefetchScalarGridSpec(
        num_scalar_prefetch=1,
        grid=(seq // q,),
        in_specs=[
            colform,
            pl.BlockSpec((LANES, q), lambda i, l: (0, i)),
            row_p, row_p, col_p, col_p,
        ],
        out_specs=[colform, colform, colform, pl.BlockSpec((LANES, q), lambda i, l: (0, i))],
    )
    col_shape = jax.ShapeDtypeStruct((seq, LANES), F32)
    return pl.pallas_call(
        _gate_kernel,
        grid_spec=gs,
        out_shape=[col_shape, col_shape, col_shape, jax.ShapeDtypeStruct((LANES, seq), F32)],
        compiler_params=_params(("parallel",)),
        name="gateprep",
    )(layer, gate, gate_t, brow, arow, bcol, acol)


def _ssd_dir(x_ref, b_ref, c_ref, c1_ref, c2_ref, c3_ref, r2_ref, s_ref, y_ref, *, d):
    q = SSD_CHUNK
    gw = SSD_WIDTH // SSD_GROUPS
    hpg = SSD_HEADS // SSD_GROUPS
    dt_all = c1_ref[...]
    cs_all = c2_ref[...]
    tot_all = c3_ref[...]
    etot = jnp.exp(tot_all[0:16])
    etot_hi = etot.astype(BF16)
    etot_lo = (etot - etot_hi.astype(F32)).astype(BF16)
    stacked = jnp.concatenate([dt_all.astype(BF16), jnp.exp(cs_all).astype(BF16),
                               jnp.exp(tot_all - cs_all).astype(BF16), etot_hi, etot_lo], axis=0)
    ri = lax.broadcasted_iota(jnp.int32, (q, q), 0)
    ci = lax.broadcasted_iota(jnp.int32, (q, q), 1)
    mask = (ri >= ci) if d == 0 else (ri <= ci)
    kk = lax.broadcasted_iota(jnp.int32, (LANES, gw), 0)
    nn = lax.broadcasted_iota(jnp.int32, (LANES, gw), 1)
    lane = lax.broadcasted_iota(jnp.int32, (q, LANES), 1)
    for g in range(SSD_GROUPS):
        base = G_SSD + SSD_HEADS * d + hpg * g
        sel = (kk == base + (nn >> 6)).astype(BF16)
        ex = _mm(stacked, sel)
        dt_e, ecs_e, dte_e = ex[0:q], ex[q:2 * q], ex[2 * q:3 * q]
        etot_e = ex[3 * q:3 * q + 1] + ex[3 * q + 16:3 * q + 17]
        xg = jnp.concatenate([x_ref[4 * g + p] for p in range(4)], axis=1).astype(F32)
        xdt = xg * dt_e
        xdt16 = xdt.astype(BF16)
        xdte16 = (xdt * dte_e).astype(BF16)
        bg = b_ref[g]
        cg = c_ref[g]
        cb = _nt(cg, bg)
        s_prev = s_ref[d, g]
        y_off = _mm(cg, s_prev.astype(BF16)) * ecs_e
        s_ref[d, g] = s_prev * etot_e + _tn(bg, xdte16)
        ys = []
        for pair in range(hpg // 2):
            ms = []
            for hh in (2 * pair, 2 * pair + 1):
                r = base + hh
                dec = jnp.exp(jnp.where(mask, cs_all[:, r:r + 1] - r2_ref[r:r + 1, :], -jnp.inf))
                ms.append((cb * dec).astype(BF16))
            lhs = jnp.concatenate(ms, axis=1)
            xp = xdt16[:, pair * LANES:(pair + 1) * LANES]
            zero = jnp.zeros_like(xp)
            rhs = jnp.concatenate([jnp.where(lane < SSD_HEAD_DIM, xp, zero),
                                   jnp.where(lane >= SSD_HEAD_DIM, xp, zero)], axis=0)
            ys.append(_mm(lhs, rhs))
        y = jnp.concatenate(ys, axis=1) + y_off
        y_ref[:, g * gw:(g + 1) * gw] = y.astype(BF16)


def _ssd_kernel(l_ref,
                xf_ref, bf_ref, cf_ref, c1f_ref, c2f_ref, c3f_ref, r2f_ref,
                xb_ref, bb_ref, cb_ref, c1b_ref, c2b_ref, c3b_ref, r2b_ref,
                yf_ref, yb_ref, s_ref):
    @pl.when(pl.program_id(0) == 0)
    def _():
        s_ref[...] = jnp.zeros_like(s_ref)

    _ssd_dir(xf_ref, bf_ref, cf_ref, c1f_ref, c2f_ref, c3f_ref, r2f_ref, s_ref, yf_ref, d=0)
    _ssd_dir(xb_ref, bb_ref, cb_ref, c1b_ref, c2b_ref, c3b_ref, r2b_ref, s_ref, yb_ref, d=1)


def _ssd_call(layer, act, c1, c2, c3, r2):
    seq = c1.shape[0]
    q = SSD_CHUNK
    nc = seq // q

    def dir_specs(rowblk):
        return [
            pl.BlockSpec((16, q, LANES), lambda i, l: (0, rowblk(i), 0)),
            pl.BlockSpec((4, q, LANES), lambda i, l: (4, rowblk(i), 0)),
            pl.BlockSpec((4, q, LANES), lambda i, l: (5, rowblk(i), 0)),
            pl.BlockSpec((q, LANES), lambda i, l: (rowblk(i), 0)),
            pl.BlockSpec((q, LANES), lambda i, l: (rowblk(i), 0)),
            pl.BlockSpec((q, LANES), lambda i, l: (rowblk(i), 0)),
            pl.BlockSpec((LANES, q), lambda i, l: (0, rowblk(i))),
        ]

    fwd = lambda i: i
    bwd = lambda i: nc - 1 - i
    gs = pltpu.PrefetchScalarGridSpec(
        num_scalar_prefetch=1,
        grid=(nc,),
        in_specs=dir_specs(fwd) + dir_specs(bwd),
        out_specs=[pl.BlockSpec((q, SSD_WIDTH), lambda i, l: (fwd(i), 0)),
                   pl.BlockSpec((q, SSD_WIDTH), lambda i, l: (bwd(i), 0))],
        scratch_shapes=[pltpu.VMEM((2, SSD_GROUPS, SSD_STATE, SSD_WIDTH // SSD_GROUPS), F32)],
    )
    y_shape = jax.ShapeDtypeStruct((seq, SSD_WIDTH), BF16)
    args = (act, act, act, c1, c2, c3, r2)
    return pl.pallas_call(
        _ssd_kernel,
        grid_spec=gs,
        out_shape=[y_shape, y_shape],
        compiler_params=_params(("arbitrary",)),
        name="ssd",
    )(layer, *args, *args)


def _gdn_prep(q_ref, k_ref, v_ref, c1_ref, c2_ref, c3_ref, r2_ref, *, j, d, head, rows):
    c = GDN_SOLVE_CHUNK
    dh = GDN_HEAD_DIM
    q = q_ref[j, rows, :].astype(F32)
    k = k_ref[j, rows, :].astype(F32)
    v = v_ref[j, rows, :].astype(F32)
    q = q * (lax.rsqrt(jnp.sum(q * q, axis=-1, keepdims=True) + EPS) * (dh ** -0.5))
    k = k * lax.rsqrt(jnp.sum(k * k, axis=-1, keepdims=True) + EPS)

    lane = lax.broadcasted_iota(jnp.int32, (c, LANES), 1)
    ig = G_GDN + GDN_HEADS * d + head
    ib = G_BETA + GDN_HEADS * d + head
    gcs = jnp.sum(jnp.where(lane == ig, c2_ref[rows, :], 0.0), axis=-1, keepdims=True)
    gtot = jnp.sum(jnp.where(lane == ig, c3_ref[rows, :], 0.0), axis=-1, keepdims=True)
    beta = jnp.sum(jnp.where(lane == ib, c1_ref[rows, :], 0.0), axis=-1, keepdims=True)
    grow = r2_ref[pl.ds(ig, 1), :][:, rows]

    ri = lax.broadcasted_iota(jnp.int32, (c, c), 0)
    ci = lax.broadcasted_iota(jnp.int32, (c, c), 1)
    incl = (ri >= ci) if d == 0 else (ri <= ci)
    strict = (ri > ci) if d == 0 else (ri < ci)
    decay = jnp.exp(jnp.where(incl, gcs - grow, -jnp.inf))

    kb = k * beta
    eg = jnp.exp(gcs)
    kq = _nt(jnp.concatenate([kb, q], axis=0).astype(BF16), k.astype(BF16))
    p = jnp.where(strict, -(kq[:c] * decay), 0.0)
    half = c // 2
    left = lax.broadcasted_iota(jnp.int32, (half, LANES), 1) < half
    return dict(
        q=jnp.where(left, p[:half], p[half:]),
        poff16=jnp.where(left, p[half:], p[:half]).astype(BF16),
        attn16=(kq[c:] * decay).astype(BF16),
        rhs16=jnp.concatenate([v * beta, kb * eg], axis=1).astype(BF16),
        qeg=q * eg,
        ket16=(k * jnp.exp(gtot - gcs)).T.astype(BF16),
        ld=jnp.exp(gtot[0:1, :]),
    )


def _gdn_kernel(l_ref,
                qf_ref, kf_ref, vf_ref, c1f_ref, c2f_ref, c3f_ref, r2f_ref,
                qb_ref, kb_ref, vb_ref, c1b_ref, c2b_ref, c3b_ref, r2b_ref,
                of_ref, ob_ref, s_ref, *, hb):
    @pl.when(pl.program_id(1) == 0)
    def _():
        s_ref[...] = jnp.zeros_like(s_ref)

    c = GDN_SOLVE_CHUNK
    dh = GDN_HEAD_DIM
    ncb = GDN_BLOCK // c
    in_refs = ((qf_ref, kf_ref, vf_ref, c1f_ref, c2f_ref, c3f_ref, r2f_ref),
               (qb_ref, kb_ref, vb_ref, c1b_ref, c2b_ref, c3b_ref, r2b_ref))
    out_refs = (of_ref, ob_ref)
    streams = [(j, d) for j in range(hb) for d in range(2)]
    state = {sd: s_ref[sd[1], sd[0]] for sd in streams}

    half = c // 2
    lane_h = lax.broadcasted_iota(jnp.int32, (half, LANES), 1)
    row_h = lax.broadcasted_iota(jnp.int32, (half, LANES), 0)
    left = lane_h < half
    eye2 = (lane_h == row_h) | (lane_h == row_h + half)

    def blockdiag16(x):
        return jnp.concatenate([jnp.where(left, x, 0.0), jnp.where(left, 0.0, x)], axis=0).astype(BF16)

    def scan_read(units):
        for (j, d), u in units.items():
            ws = _mm(u["lhs16"], state[(j, d)].astype(BF16))
            u["v_new16"] = (u["u"] - ws[:c]).astype(BF16)
            out_refs[d][j, u["rows"], :] = (ws[c:] + u["au"]).astype(BF16)

    def scan_write(units):
        for sd, u in units.items():
            state[sd] = state[sd] * u["ld"] + _mm(u["ket16"], u["v_new16"])

    prev = None
    for phase in range(ncb):
        units = {}
        for (j, d) in streams:
            cc = phase if d == 0 else ncb - 1 - phase
            rows = slice(cc * c, (cc + 1) * c)
            units[(j, d)] = _gdn_prep(*in_refs[d], j=j, d=d, head=pl.program_id(0) * hb + j, rows=rows)
            units[(j, d)]["rows"] = rows
        woven = [] if prev is None else [functools.partial(scan_read, prev), functools.partial(scan_write, prev)]
        for u in units.values():
            u["t"] = jnp.where(eye2, 1.0, u["q"])
            u["q"] = _mm(u["q"].astype(BF16), blockdiag16(u["q"]))
        if woven:
            woven[0]()
        for level in range(1, GDN_INV_SHIFT):
            for u in units.values():
                rhs = blockdiag16(u["q"])
                if level == GDN_INV_SHIFT - 1:
                    u["t"] = u["t"] + _mm(u["t"].astype(BF16), rhs)
                else:
                    qt = _mm(jnp.concatenate([u["q"], u["t"]], axis=0).astype(BF16), rhs)
                    u["q"] = qt[:half]
                    u["t"] = u["t"] + qt[half:]
            if level < len(woven):
                woven[level]()
        for u in units.values():
            u["tbd16"] = blockdiag16(u["t"])
            u["y16"] = _mm(u["poff16"], u["tbd16"]).astype(BF16)
        zeros = jnp.zeros((half, LANES), BF16)
        for (j, d), u in units.items():
            tbd16 = u["tbd16"]
            if d == 0:
                z = _mm(tbd16[half:], jnp.concatenate([zeros, u["y16"]], axis=0))
                u["t16"] = jnp.concatenate([tbd16[:half], jnp.where(left, z, u["t"]).astype(BF16)], axis=0)
            else:
                z = _mm(tbd16[:half], jnp.concatenate([u["y16"], zeros], axis=0))
                u["t16"] = jnp.concatenate([jnp.where(left, u["t"], z).astype(BF16), tbd16[half:]], axis=0)
        for u in units.values():
            u["sol"] = _mm(u["t16"], u["rhs16"])
        for u in units.values():
            aw = _mm(u["attn16"], u["sol"].astype(BF16))
            u["u"] = u["sol"][:, :dh]
            u["au"] = aw[:, :dh]
            u["lhs16"] = jnp.concatenate([u["sol"][:, dh:], u["qeg"] - aw[:, dh:]], axis=0).astype(BF16)
        prev = units
    scan_read(prev)
    scan_write(prev)
    for (j, d) in streams:
        s_ref[d, j] = state[(j, d)]


def _gdn_call(layer, act, c1, c2, c3, r2, *, hb=8):
    seq = c1.shape[0]
    r = GDN_BLOCK
    nb = seq // r
    q0, k0, v0 = 24 // hb, 40 // hb, 56 // hb

    def dir_specs(rowblk):
        return [
            pl.BlockSpec((hb, r, LANES), lambda h, i, l: (q0 + h, rowblk(i), 0)),
            pl.BlockSpec((hb, r, LANES), lambda h, i, l: (k0 + h, rowblk(i), 0)),
            pl.BlockSpec((hb, r, LANES), lambda h, i, l: (v0 + h, rowblk(i), 0)),
            pl.BlockSpec((r, LANES), lambda h, i, l: (rowblk(i), 0)),
            pl.BlockSpec((r, LANES), lambda h, i, l: (rowblk(i), 0)),
            pl.BlockSpec((r, LANES), lambda h, i, l: (rowblk(i), 0)),
            pl.BlockSpec((LANES, r), lambda h, i, l: (0, rowblk(i))),
        ]

    fwd = lambda i: i
    bwd = lambda i: nb - 1 - i
    gs = pltpu.PrefetchScalarGridSpec(
        num_scalar_prefetch=1,
        grid=(GDN_HEADS // hb, nb),
        in_specs=dir_specs(fwd) + dir_specs(bwd),
        out_specs=[pl.BlockSpec((hb, r, LANES), lambda h, i, l: (h, fwd(i), 0)),
                   pl.BlockSpec((hb, r, LANES), lambda h, i, l: (h, bwd(i), 0))],
        scratch_shapes=[pltpu.VMEM((2, hb, GDN_HEAD_DIM, GDN_HEAD_DIM), F32)],
    )
    o_shape = jax.ShapeDtypeStruct((GDN_HEADS, seq, LANES), BF16)
    args = (act, act, act, c1, c2, c3, r2)
    return pl.pallas_call(
        functools.partial(_gdn_kernel, hb=hb),
        grid_spec=gs,
        out_shape=[o_shape, o_shape],
        compiler_params=_params(("parallel", "arbitrary")),
        name="gdn",
    )(layer, *args, *args)


def _outproj_kernel(l_ref, x_ref, yf_ref, yb_ref, xa_ref, zs_ref, zg_ref, of_ref, ob_ref,
                    dexp_ref, sg_ref, gg_ref, w_ref, post_ref, o_ref):
    gw = SSD_WIDTH // SSD_GROUPS
    hq = gw // LANES
    acc = None
    for g in range(SSD_GROUPS):
        sl = slice(g * gw, (g + 1) * gw)
        xg = jnp.concatenate([xa_ref[hq * g + p] for p in range(hq)], axis=1).astype(F32)
        y = yf_ref[:, sl].astype(F32) + yb_ref[:, sl].astype(F32) + dexp_ref[:, sl] * xg
        y = y * _silu(zs_ref[:, sl].astype(F32))
        part = _mm(_rms(y, sg_ref[:, sl]).astype(BF16), w_ref[sl, :])
        acc = part if acc is None else acc + part
    for g in range(GDN_WIDTH // gw):
        os = []
        for h in range(hq * g, hq * (g + 1)):
            o = of_ref[h].astype(F32) + ob_ref[h].astype(F32)
            o = _rms(o, gg_ref[...]) * _silu(zg_ref[:, h * LANES:(h + 1) * LANES].astype(F32))
            os.append(o.astype(BF16))
        acc = acc + _mm(jnp.concatenate(os, axis=1), w_ref[SSD_WIDTH + g * gw:SSD_WIDTH + (g + 1) * gw, :])
    o_ref[...] = x_ref[...] + _rms(acc, post_ref[...])


def _outproj_call(layer, x, yf, yb, act, z, of, ob, dexp, ssd_g, gdn_g, w_out, post_g, *, tm=256):
    seq, d = x.shape
    row_d = pl.BlockSpec((None, 1, d), lambda i, l: (l[0], 0, 0))
    gs = pltpu.PrefetchScalarGridSpec(
        num_scalar_prefetch=1,
        grid=(seq // tm,),
        in_specs=[
            pl.BlockSpec((tm, d), lambda i, l: (i, 0)),
            pl.BlockSpec((tm, SSD_WIDTH), lambda i, l: (i, 0)),
            pl.BlockSpec((tm, SSD_WIDTH), lambda i, l: (i, 0)),
            pl.BlockSpec((16, tm, LANES), lambda i, l: (0, i, 0)),
            pl.BlockSpec((tm, SSD_WIDTH), lambda i, l: (i, COL_ZS // SSD_WIDTH)),
            pl.BlockSpec((tm, GDN_WIDTH), lambda i, l: (i, COL_ZG // GDN_WIDTH)),
            pl.BlockSpec((GDN_HEADS, tm, LANES), lambda i, l: (0, i, 0)),
            pl.BlockSpec((GDN_HEADS, tm, LANES), lambda i, l: (0, i, 0)),
            row_d, row_d,
            pl.BlockSpec((None, 1, LANES), lambda i, l: (l[0], 0, 0)),
            pl.BlockSpec((None, 2 * d, d), lambda i, l: (l[0], 0, 0), pipeline_mode=pl.Buffered(1)),
            row_d,
        ],
        out_specs=pl.BlockSpec((tm, d), lambda i, l: (i, 0)),
    )
    return pl.pallas_call(
        _outproj_kernel,
        grid_spec=gs,
        out_shape=jax.ShapeDtypeStruct((seq, d), F32),
        compiler_params=_params(("parallel",)),
        name="outproj",
    )(layer, x, yf, yb, act, z, z, of, ob, dexp, ssd_g, gdn_g, w_out, post_g)


def kernel(x, ffn1_pre_g, ffn1_w_gu, ffn1_w_down, ffn1_post_g, mix_pre_g, w_in, ssd_conv_w, ssd_conv_b,
           ssd_dt_bias, ssd_a_log, ssd_d, ssd_norm_g, gdn_conv_w, gdn_dt_bias, gdn_a_log, gdn_norm_g, w_out,
           mix_post_g, ffn2_pre_g, ffn2_w_gu, ffn2_w_down, ffn2_post_g):
    row = lambda g: g.astype(F32).reshape(DEPTH, 1, -1)

    s0, s1, s2 = SSD_WIDTH, SSD_WIDTH + SSD_CONV_CH, SSD_WIDTH + SSD_CONV_CH + 2 * SSD_HEADS
    g1, g2 = s2 + GDN_CONV_CH, s2 + GDN_CONV_CH + GDN_WIDTH
    w_in16 = lax.optimization_barrier(w_in.astype(BF16))
    w_in_p = jnp.concatenate([w_in16[..., a:b]
                              for a, b in ((0, s0), (g1, g2), (s0, s1), (s2, g1), (s1, s2), (g2, D_IN_PROJ))], axis=-1)
    conv_w = jnp.concatenate([ssd_conv_w, gdn_conv_w], axis=-1).astype(F32)
    conv_b = jnp.concatenate([ssd_conv_b.astype(F32), jnp.zeros((DEPTH, GDN_CONV_CH), F32)], axis=-1)[:, None, :]
    pad = jnp.zeros((DEPTH, LANES - G_BETA), F32)
    bias = jnp.concatenate([ssd_dt_bias.reshape(DEPTH, -1), gdn_dt_bias.reshape(DEPTH, -1), pad], axis=-1)
    alog = jnp.concatenate([ssd_a_log.reshape(DEPTH, -1), gdn_a_log.reshape(DEPTH, -1), pad], axis=-1)
    brow, arow = bias[:, None, :], alog[:, None, :]
    bcol, acol = bias[:, :, None], alog[:, :, None]
    dexp = jnp.repeat(ssd_d.astype(F32), SSD_HEAD_DIM, axis=-1)[:, None, :]
    f1_gu, f1_dn = ffn1_w_gu.astype(BF16), ffn1_w_down.astype(BF16)
    f2_gu, f2_dn = ffn2_w_gu.astype(BF16), ffn2_w_down.astype(BF16)
    w_out16 = w_out.astype(BF16)
    f1_pre, f1_post, f2_pre, f2_post = row(ffn1_pre_g), row(ffn1_post_g), row(ffn2_pre_g), row(ffn2_post_g)
    m_pre, m_post, ssd_g, gdn_g = row(mix_pre_g), row(mix_post_g), row(ssd_norm_g), row(gdn_norm_g)

    def layer_fn(i, xc):
        layer = jnp.reshape(i, (1,)).astype(jnp.int32)
        xc = _ffn_call(layer, xc, f1_pre, f1_gu, f1_dn, f1_post)
        z, gate, act = _inproj_call(layer, xc, m_pre, w_in_p, conv_w, conv_b)
        c1, c2, c3, r2 = _gate_call(layer, gate, gate.T, brow, arow, bcol, acol)
        yf, yb = _ssd_call(layer, act, c1, c2, c3, r2)
        of, ob = _gdn_call(layer, act, c1, c2, c3, r2)
        xc = _outproj_call(layer, xc, yf, yb, act, z, of, ob, dexp, ssd_g, gdn_g, w_out16, m_post)
        xc = _ffn_call(layer, xc, f2_pre, f2_gu, f2_dn, f2_post)
        return xc

    out = lax.fori_loop(0, DEPTH, layer_fn, x.reshape(SEQ, D_MODEL).astype(F32))
    return out.reshape(x.shape)
```

```python
import functools

import jax
import jax.numpy as jnp
from jax import lax
from jax.experimental import pallas as pl
from jax.experimental.pallas import tpu as pltpu

F32 = jnp.float32
BF16 = jnp.bfloat16
HIGHEST = lax.Precision.HIGHEST

D_MODEL = 2048
SEQ = 8192
DEPTH = 4
EPS = 1e-6
D_FF = 5632
CONV_WIDTH = 5
SSD_HEADS = 32
SSD_HEAD_DIM = 64
SSD_WIDTH = 2048
SSD_GROUPS = 4
SSD_STATE = 128
SSD_CHUNK = 128
SSD_BC = 512
SSD_CONV_CH = SSD_WIDTH + 2 * SSD_BC
GDN_HEADS = 16
GDN_HEAD_DIM = 128
GDN_WIDTH = 2048
GDN_CHUNK = 64
GDN_CONV_CH = 3 * GDN_WIDTH
D_IN_PROJ = 13440

LANES = 128
COL_ZS = 0
COL_ZG = 2048
COL_CONV = 4096
N_CONV = SSD_CONV_CH + GDN_CONV_CH
COL_GATE = COL_CONV + N_CONV
N_CONV_BLK = N_CONV // LANES
G_SSD = 0
G_GDN = 64
G_BETA = 96

GDN_SOLVE_CHUNK = 128
GDN_INV_SHIFT = 6
GDN_BLOCK = 512
assert 2 << GDN_INV_SHIFT == GDN_SOLVE_CHUNK
assert GDN_SOLVE_CHUNK == SSD_CHUNK
VMEM_LIMIT = 56 * 1024 * 1024


def _mm(a, b, precision=None):
    return jnp.dot(a, b, preferred_element_type=F32, precision=precision)


def _nt(a, b, precision=None):
    return lax.dot_general(a, b, (((1,), (1,)), ((), ())), preferred_element_type=F32, precision=precision)


def _tn(a, b):
    return lax.dot_general(a, b, (((0,), (0,)), ((), ())), preferred_element_type=F32)


def _rms(x, g):
    return x * lax.rsqrt(jnp.mean(x * x, axis=-1, keepdims=True) + EPS) * g


def _silu(x):
    return x * jax.nn.sigmoid(x)


def _softplus(x):
    return jnp.maximum(x, 0.0) + jnp.log1p(jnp.exp(-jnp.abs(x)))


def _params(sem):
    return pltpu.CompilerParams(dimension_semantics=sem, vmem_limit_bytes=VMEM_LIMIT)


def _ffn_kernel(l_ref, x_ref, pre_ref, wg_ref, wu_ref, wd_ref, post_ref, o_ref, h_ref, acc_ref, *, nj):
    j = pl.program_id(1)

    @pl.when(j == 0)
    def _():
        h_ref[...] = _rms(x_ref[...], pre_ref[...]).astype(BF16)
        acc_ref[...] = jnp.zeros_like(acc_ref)

    h = h_ref[...]
    gate = _mm(h, wg_ref[...])
    up = _mm(h, wu_ref[...])
    act = (_silu(gate) * up).astype(BF16)
    acc_ref[...] += _mm(act, wd_ref[...])

    @pl.when(j == nj - 1)
    def _():
        o_ref[...] = x_ref[...] + 0.5 * _rms(acc_ref[...], post_ref[...])


def _ffn_call(layer, x, pre_g, w_gu, w_down, post_g, *, tm=512, tf=512):
    seq, d = x.shape
    nj = D_FF // tf
    gs = pltpu.PrefetchScalarGridSpec(
        num_scalar_prefetch=1,
        grid=(seq // tm, nj),
        in_specs=[
            pl.BlockSpec((tm, d), lambda i, j, l: (i, 0)),
            pl.BlockSpec((None, 1, d), lambda i, j, l: (l[0], 0, 0)),
            pl.BlockSpec((None, d, tf), lambda i, j, l: (l[0], 0, j)),
            pl.BlockSpec((None, d, tf), lambda i, j, l: (l[0], 0, j + nj)),
            pl.BlockSpec((None, tf, d), lambda i, j, l: (l[0], j, 0)),
            pl.BlockSpec((None, 1, d), lambda i, j, l: (l[0], 0, 0)),
        ],
        out_specs=pl.BlockSpec((tm, d), lambda i, j, l: (i, 0)),
        scratch_shapes=[pltpu.VMEM((tm, d), BF16), pltpu.VMEM((tm, d), F32)],
    )
    return pl.pallas_call(
        functools.partial(_ffn_kernel, nj=nj),
        grid_spec=gs,
        out_shape=jax.ShapeDtypeStruct((seq, d), F32),
        compiler_params=_params(("parallel", "arbitrary")),
        name="ffn",
    )(layer, x, pre_g, w_gu, w_gu, w_down, post_g)


HALO = 16
CONV_SUB = 256


def _gate_block(raw, brow, arow, bcol, acol):
    q = SSD_CHUNK
    lane = lax.broadcasted_iota(jnp.int32, (q, LANES), 1)
    sp = _softplus(raw + brow)
    c1 = jnp.where(lane < G_BETA, sp, jax.nn.sigmoid(raw))
    a = jnp.where(lane < G_BETA, sp * (-jnp.exp(arow)), 0.0)

    ri = lax.broadcasted_iota(jnp.int32, (q, q), 0)
    ci = lax.broadcasted_iota(jnp.int32, (q, q), 1)
    lower = (ri >= ci).astype(F32)
    upper = (ri <= ci).astype(F32)
    ones = jnp.ones((q, q), F32)

    def is_fwd(idx):
        return (idx < G_SSD + SSD_HEADS) | ((idx >= G_GDN) & (idx < G_GDN + GDN_HEADS))

    c2 = jnp.where(is_fwd(lane), _mm(lower, a, HIGHEST), _mm(upper, a, HIGHEST))
    c3 = _mm(ones, a, HIGHEST)
    sub = lax.broadcasted_iota(jnp.int32, (LANES, q), 0)
    at = jnp.where(sub < G_BETA, _softplus(raw.T + bcol) * (-jnp.exp(acol)), 0.0)
    r2 = jnp.where(is_fwd(sub), _nt(at, lower, HIGHEST), _nt(at, upper, HIGHEST))
    return c1, c2, c3, r2


def _inproj_kernel(l_ref, xp_ref, x_ref, xn_ref, pre_ref, w_ref, wg_ref, cw_ref, cb_ref,
                   brow_ref, arow_ref, bcol_ref, acol_ref,
                   z_ref, c1_ref, c2_ref, c3_ref, r2_ref, act_ref, h_ref, *, tm, ni, nz):
    i = pl.program_id(0)
    j = pl.program_id(1)

    @pl.when(j == 0)
    def _():
        pre = pre_ref[...]
        h_ref[0:HALO, :] = jnp.where(i > 0, _rms(xp_ref[...], pre), 0.0).astype(BF16)
        h_ref[HALO:HALO + tm, :] = _rms(x_ref[...], pre).astype(BF16)
        h_ref[HALO + tm:, :] = jnp.where(i < ni - 1, _rms(xn_ref[...], pre), 0.0).astype(BF16)
        gate = _mm(h_ref[HALO:HALO + tm, :], wg_ref[...])
        for c in range(tm // SSD_CHUNK):
            rows = slice(c * SSD_CHUNK, (c + 1) * SSD_CHUNK)
            c1, c2, c3, r2 = _gate_block(gate[rows], brow_ref[...], arow_ref[...], bcol_ref[...], acol_ref[...])
            c1_ref[rows, :] = c1
            c2_ref[rows, :] = c2
            c3_ref[rows, :] = c3
            r2_ref[:, rows] = r2

    @pl.when(j < nz)
    def _():
        z_ref[...] = _mm(h_ref[HALO:HALO + tm, :], w_ref[...]).astype(BF16)

    @pl.when(j >= nz)
    def _():
        pad = CONV_WIDTH // 2
        h = h_ref[...]
        for s in range(w_ref.shape[1] // CONV_SUB):
            cols = slice(s * CONV_SUB, (s + 1) * CONV_SUB)
            acc = _mm(h, w_ref[:, cols])
            y = cb_ref[:, cols]
            for t in range(CONV_WIDTH):
                y = y + cw_ref[t:t + 1, cols] * acc[HALO - pad + t:HALO - pad + t + tm, :]
            y = _silu(y).astype(BF16)
            for b in range(CONV_SUB // LANES):
                act_ref[s * (CONV_SUB // LANES) + b] = y[:, b * LANES:(b + 1) * LANES]


def _inproj_call(layer, x, pre_g, w_in, conv_w, conv_b, brow, arow, bcol, acol, *, tm=1024, tn=1024):
    seq, d = x.shape
    ni = seq // tm
    nz = COL_CONV // tn
    nj = COL_GATE // tn
    hb = tm // HALO
    row_p = pl.BlockSpec((None, 1, LANES), lambda i, j, l: (l[0], 0, 0))
    col_p = pl.BlockSpec((None, LANES, 1), lambda i, j, l: (l[0], 0, 0))
    colform = pl.BlockSpec((tm, LANES), lambda i, j, l: (i, 0))
    col_shape = jax.ShapeDtypeStruct((seq, LANES), F32)
    gs = pltpu.PrefetchScalarGridSpec(
        num_scalar_prefetch=1,
        grid=(ni, nj),
        in_specs=[
            pl.BlockSpec((HALO, d), lambda i, j, l: (jnp.maximum(i * hb - 1, 0), 0)),
            pl.BlockSpec((tm, d), lambda i, j, l: (i, 0)),
            pl.BlockSpec((HALO, d), lambda i, j, l: (jnp.minimum((i + 1) * hb, seq // HALO - 1), 0)),
            pl.BlockSpec((None, 1, d), lambda i, j, l: (l[0], 0, 0)),
            pl.BlockSpec((None, d, tn), lambda i, j, l: (l[0], 0, j)),
            pl.BlockSpec((None, d, LANES), lambda i, j, l: (l[0], 0, COL_GATE // LANES)),
            pl.BlockSpec((None, CONV_WIDTH, tn), lambda i, j, l: (l[0], 0, jnp.maximum(j - nz, 0))),
            pl.BlockSpec((None, 1, tn), lambda i, j, l: (l[0], 0, jnp.maximum(j - nz, 0))),
            row_p, row_p, col_p, col_p,
        ],
        out_specs=[
            pl.BlockSpec((tm, tn), lambda i, j, l: (i, jnp.minimum(j, nz - 1))),
            colform, colform, colform,
            pl.BlockSpec((LANES, tm), lambda i, j, l: (0, i)),
            pl.BlockSpec((tn // LANES, tm, LANES), lambda i, j, l: (jnp.maximum(j - nz, 0), i, 0)),
        ],
        scratch_shapes=[pltpu.VMEM((tm + 2 * HALO, d), BF16)],
    )
    return pl.pallas_call(
        functools.partial(_inproj_kernel, tm=tm, ni=ni, nz=nz),
        grid_spec=gs,
        out_shape=[jax.ShapeDtypeStruct((seq, COL_CONV), BF16),
                   col_shape, col_shape, col_shape, jax.ShapeDtypeStruct((LANES, seq), F32),
                   jax.ShapeDtypeStruct((N_CONV_BLK, seq, LANES), BF16)],
        compiler_params=_params(("parallel", "arbitrary")),
        name="inproj",
    )(layer, x, x, x, pre_g, w_in, w_in, conv_w, conv_b, brow, arow, bcol, acol)


def _ssd_dir(x_ref, b_ref, c_ref, c1_ref, c2_ref, c3_ref, r2_ref, s_ref, y_ref, *, d):
    q = SSD_CHUNK
    gw = SSD_WIDTH // SSD_GROUPS
    hpg = SSD_HEADS // SSD_GROUPS
    dt_all = c1_ref[...]
    cs_all = c2_ref[...]
    tot_all = c3_ref[...]
    etot = jnp.exp(tot_all[0:16])
    etot_hi = etot.astype(BF16)
    etot_lo = (etot - etot_hi.astype(F32)).astype(BF16)
    stacked = jnp.concatenate([dt_all.astype(BF16), jnp.exp(cs_all).astype(BF16),
                               jnp.exp(tot_all - cs_all).astype(BF16), etot_hi, etot_lo], axis=0)
    ri = lax.broadcasted_iota(jnp.int32, (q, q), 0)
    ci = lax.broadcasted_iota(jnp.int32, (q, q), 1)
    mask = (ri >= ci) if d == 0 else (ri <= ci)
    kk = lax.broadcasted_iota(jnp.int32, (LANES, gw), 0)
    nn = lax.broadcasted_iota(jnp.int32, (LANES, gw), 1)
    lane = lax.broadcasted_iota(jnp.int32, (q, LANES), 1)
    for g in range(SSD_GROUPS):
        base = G_SSD + SSD_HEADS * d + hpg * g
        sel = (kk == base + (nn >> 6)).astype(BF16)
        ex = _mm(stacked, sel)
        dt_e, ecs_e, dte_e = ex[0:q], ex[q:2 * q], ex[2 * q:3 * q]
        etot_e = ex[3 * q:3 * q + 1] + ex[3 * q + 16:3 * q + 17]
        xg = jnp.concatenate([x_ref[4 * g + p] for p in range(4)], axis=1).astype(F32)
        xdt = xg * dt_e
        xdt16 = xdt.astype(BF16)
        xdte16 = (xdt * dte_e).astype(BF16)
        bg = b_ref[g]
        cg = c_ref[g]
        cb = _nt(cg, bg)
        s_prev = s_ref[d, g]
        y_off = _mm(cg, s_prev.astype(BF16)) * ecs_e
        s_ref[d, g] = s_prev * etot_e + _tn(bg, xdte16)
        ys = []
        for pair in range(hpg // 2):
            ms = []
            for hh in (2 * pair, 2 * pair + 1):
                r = base + hh
                dec = jnp.exp(jnp.where(mask, cs_all[:, r:r + 1] - r2_ref[r:r + 1, :], -jnp.inf))
                ms.append((cb * dec).astype(BF16))
            lhs = jnp.concatenate(ms, axis=1)
            xp = xdt16[:, pair * LANES:(pair + 1) * LANES]
            zero = jnp.zeros_like(xp)
            rhs = jnp.concatenate([jnp.where(lane < SSD_HEAD_DIM, xp, zero),
                                   jnp.where(lane >= SSD_HEAD_DIM, xp, zero)], axis=0)
            ys.append(_mm(lhs, rhs))
        y = jnp.concatenate(ys, axis=1) + y_off
        y_ref[:, g * gw:(g + 1) * gw] = y.astype(BF16)


def _ssd_kernel(l_ref,
                xf_ref, bf_ref, cf_ref, c1f_ref, c2f_ref, c3f_ref, r2f_ref,
                xb_ref, bb_ref, cb_ref, c1b_ref, c2b_ref, c3b_ref, r2b_ref,
                yf_ref, yb_ref, s_ref):
    @pl.when(pl.program_id(0) == 0)
    def _():
        s_ref[...] = jnp.zeros_like(s_ref)

    _ssd_dir(xf_ref, bf_ref, cf_ref, c1f_ref, c2f_ref, c3f_ref, r2f_ref, s_ref, yf_ref, d=0)
    _ssd_dir(xb_ref, bb_ref, cb_ref, c1b_ref, c2b_ref, c3b_ref, r2b_ref, s_ref, yb_ref, d=1)


def _ssd_call(layer, act, c1, c2, c3, r2):
    seq = c1.shape[0]
    q = SSD_CHUNK
    nc = seq // q

    def dir_specs(rowblk):
        return [
            pl.BlockSpec((16, q, LANES), lambda i, l: (0, rowblk(i), 0)),
            pl.BlockSpec((4, q, LANES), lambda i, l: (4, rowblk(i), 0)),
            pl.BlockSpec((4, q, LANES), lambda i, l: (5, rowblk(i), 0)),
            pl.BlockSpec((q, LANES), lambda i, l: (rowblk(i), 0)),
            pl.BlockSpec((q, LANES), lambda i, l: (rowblk(i), 0)),
            pl.BlockSpec((q, LANES), lambda i, l: (rowblk(i), 0)),
            pl.BlockSpec((LANES, q), lambda i, l: (0, rowblk(i))),
        ]

    fwd = lambda i: i
    bwd = lambda i: nc - 1 - i
    gs = pltpu.PrefetchScalarGridSpec(
        num_scalar_prefetch=1,
        grid=(nc,),
        in_specs=dir_specs(fwd) + dir_specs(bwd),
        out_specs=[pl.BlockSpec((q, SSD_WIDTH), lambda i, l: (fwd(i), 0)),
                   pl.BlockSpec((q, SSD_WIDTH), lambda i, l: (bwd(i), 0))],
        scratch_shapes=[pltpu.VMEM((2, SSD_GROUPS, SSD_STATE, SSD_WIDTH // SSD_GROUPS), F32)],
    )
    y_shape = jax.ShapeDtypeStruct((seq, SSD_WIDTH), BF16)
    args = (act, act, act, c1, c2, c3, r2)
    return pl.pallas_call(
        _ssd_kernel,
        grid_spec=gs,
        out_shape=[y_shape, y_shape],
        compiler_params=_params(("arbitrary",)),
        name="ssd",
    )(layer, *args, *args)


def _gdn_prep(q_ref, k_ref, v_ref, c1_ref, c2_ref, c3_ref, r2_ref, *, j, d, head, rows):
    c = GDN_SOLVE_CHUNK
    dh = GDN_HEAD_DIM
    q = q_ref[j, rows, :].astype(F32)
    k = k_ref[j, rows, :].astype(F32)
    v = v_ref[j, rows, :].astype(F32)
    q = q * (lax.rsqrt(jnp.sum(q * q, axis=-1, keepdims=True) + EPS) * (dh ** -0.5))
    k = k * lax.rsqrt(jnp.sum(k * k, axis=-1, keepdims=True) + EPS)

    lane = lax.broadcasted_iota(jnp.int32, (c, LANES), 1)
    ig = G_GDN + GDN_HEADS * d + head
    ib = G_BETA + GDN_HEADS * d + head
    gcs = jnp.sum(jnp.where(lane == ig, c2_ref[rows, :], 0.0), axis=-1, keepdims=True)
    gtot = jnp.sum(jnp.where(lane == ig, c3_ref[rows, :], 0.0), axis=-1, keepdims=True)
    beta = jnp.sum(jnp.where(lane == ib, c1_ref[rows, :], 0.0), axis=-1, keepdims=True)
    grow = r2_ref[pl.ds(ig, 1), :][:, rows]

    ri = lax.broadcasted_iota(jnp.int32, (c, c), 0)
    ci = lax.broadcasted_iota(jnp.int32, (c, c), 1)
    incl = (ri >= ci) if d == 0 else (ri <= ci)
    strict = (ri > ci) if d == 0 else (ri < ci)
    decay = jnp.exp(jnp.where(incl, gcs - grow, -jnp.inf))

    kb = k * beta
    eg = jnp.exp(gcs)
    kq = _nt(jnp.concatenate([kb, q], axis=0).astype(BF16), k.astype(BF16))
    p = jnp.where(strict, -(kq[:c] * decay), 0.0)
    half = c // 2
    left = lax.broadcasted_iota(jnp.int32, (half, LANES), 1) < half
    return dict(
        q=jnp.where(left, p[:half], p[half:]),
        poff16=jnp.where(left, p[half:], p[:half]).astype(BF16),
        attn16=(kq[c:] * decay).astype(BF16),
        rhs16=jnp.concatenate([v * beta, kb * eg], axis=1).astype(BF16),
        qeg=q * eg,
        ket16=(k * jnp.exp(gtot - gcs)).T.astype(BF16),
        ld=jnp.exp(gtot[0:1, :]),
    )


def _gdn_kernel(l_ref,
                qf_ref, kf_ref, vf_ref, c1f_ref, c2f_ref, c3f_ref, r2f_ref,
                qb_ref, kb_ref, vb_ref, c1b_ref, c2b_ref, c3b_ref, r2b_ref,
                of_ref, ob_ref, s_ref, *, hb):
    @pl.when(pl.program_id(1) == 0)
    def _():
        s_ref[...] = jnp.zeros_like(s_ref)

    c = GDN_SOLVE_CHUNK
    dh = GDN_HEAD_DIM
    ncb = GDN_BLOCK // c
    in_refs = ((qf_ref, kf_ref, vf_ref, c1f_ref, c2f_ref, c3f_ref, r2f_ref),
               (qb_ref, kb_ref, vb_ref, c1b_ref, c2b_ref, c3b_ref, r2b_ref))
    out_refs = (of_ref, ob_ref)
    streams = [(j, d) for j in range(hb) for d in range(2)]
    state = {sd: s_ref[sd[1], sd[0]] for sd in streams}

    half = c // 2
    lane_h = lax.broadcasted_iota(jnp.int32, (half, LANES), 1)
    row_h = lax.broadcasted_iota(jnp.int32, (half, LANES), 0)
    left = lane_h < half
    eye2 = (lane_h == row_h) | (lane_h == row_h + half)

    def blockdiag16(x):
        return jnp.concatenate([jnp.where(left, x, 0.0), jnp.where(left, 0.0, x)], axis=0).astype(BF16)

    def scan_read(units):
        for (j, d), u in units.items():
            ws = _mm(u["lhs16"], state[(j, d)].astype(BF16))
            u["v_new16"] = (u["u"] - ws[:c]).astype(BF16)
            out_refs[d][j, u["rows"], :] = (ws[c:] + u["au"]).astype(BF16)

    def scan_write(units):
        for sd, u in units.items():
            state[sd] = state[sd] * u["ld"] + _mm(u["ket16"], u["v_new16"])

    prev = None
    for phase in range(ncb):
        units = {}
        for (j, d) in streams:
            cc = phase if d == 0 else ncb - 1 - phase
            rows = slice(cc * c, (cc + 1) * c)
            units[(j, d)] = _gdn_prep(*in_refs[d], j=j, d=d, head=pl.program_id(0) * hb + j, rows=rows)
            units[(j, d)]["rows"] = rows
        woven = [] if prev is None else [functools.partial(scan_read, prev), functools.partial(scan_write, prev)]
        for u in units.values():
            u["t"] = jnp.where(eye2, 1.0, u["q"])
            u["q"] = _mm(u["q"].astype(BF16), blockdiag16(u["q"]))
        if woven:
            woven[0]()
        for level in range(1, GDN_INV_SHIFT):
            for u in units.values():
                rhs = blockdiag16(u["q"])
                if level == GDN_INV_SHIFT - 1:
                    u["t"] = u["t"] + _mm(u["t"].astype(BF16), rhs)
                else:
                    qt = _mm(jnp.concatenate([u["q"], u["t"]], axis=0).astype(BF16), rhs)
                    u["q"] = qt[:half]
                    u["t"] = u["t"] + qt[half:]
            if level < len(woven):
                woven[level]()
        for u in units.values():
            u["tbd16"] = blockdiag16(u["t"])
            u["y16"] = _mm(u["poff16"], u["tbd16"]).astype(BF16)
        zeros = jnp.zeros((half, LANES), BF16)
        for (j, d), u in units.items():
            tbd16 = u["tbd16"]
            if d == 0:
                z = _mm(tbd16[half:], jnp.concatenate([zeros, u["y16"]], axis=0))
                u["t16"] = jnp.concatenate([tbd16[:half], jnp.where(left, z, u["t"]).astype(BF16)], axis=0)
            else:
                z = _mm(tbd16[:half], jnp.concatenate([u["y16"], zeros], axis=0))
                u["t16"] = jnp.concatenate([jnp.where(left, u["t"], z).astype(BF16), tbd16[half:]], axis=0)
        for u in units.values():
            u["sol"] = _mm(u["t16"], u["rhs16"])
        for u in units.values():
            aw = _mm(u["attn16"], u["sol"].astype(BF16))
            u["u"] = u["sol"][:, :dh]
            u["au"] = aw[:, :dh]
            u["lhs16"] = jnp.concatenate([u["sol"][:, dh:], u["qeg"] - aw[:, dh:]], axis=0).astype(BF16)
        prev = units
    scan_read(prev)
    scan_write(prev)
    for (j, d) in streams:
        s_ref[d, j] = state[(j, d)]


def _gdn_call(layer, act, c1, c2, c3, r2, *, hb=8):
    seq = c1.shape[0]
    r = GDN_BLOCK
    nb = seq // r
    q0, k0, v0 = 24 // hb, 40 // hb, 56 // hb

    def dir_specs(rowblk):
        return [
            pl.BlockSpec((hb, r, LANES), lambda h, i, l: (q0 + h, rowblk(i), 0)),
            pl.BlockSpec((hb, r, LANES), lambda h, i, l: (k0 + h, rowblk(i), 0)),
            pl.BlockSpec((hb, r, LANES), lambda h, i, l: (v0 + h, rowblk(i), 0)),
            pl.BlockSpec((r, LANES), lambda h, i, l: (rowblk(i), 0)),
            pl.BlockSpec((r, LANES), lambda h, i, l: (rowblk(i), 0)),
            pl.BlockSpec((r, LANES), lambda h, i, l: (rowblk(i), 0)),
            pl.BlockSpec((LANES, r), lambda h, i, l: (0, rowblk(i))),
        ]

    fwd = lambda i: i
    bwd = lambda i: nb - 1 - i
    gs = pltpu.PrefetchScalarGridSpec(
        num_scalar_prefetch=1,
        grid=(GDN_HEADS // hb, nb),
        in_specs=dir_specs(fwd) + dir_specs(bwd),
        out_specs=[pl.BlockSpec((hb, r, LANES), lambda h, i, l: (h, fwd(i), 0)),
                   pl.BlockSpec((hb, r, LANES), lambda h, i, l: (h, bwd(i), 0))],
        scratch_shapes=[pltpu.VMEM((2, hb, GDN_HEAD_DIM, GDN_HEAD_DIM), F32)],
    )
    o_shape = jax.ShapeDtypeStruct((GDN_HEADS, seq, LANES), BF16)
    args = (act, act, act, c1, c2, c3, r2)
    return pl.pallas_call(
        functools.partial(_gdn_kernel, hb=hb),
        grid_spec=gs,
        out_shape=[o_shape, o_shape],
        compiler_params=_params(("parallel", "arbitrary")),
        name="gdn",
    )(layer, *args, *args)


def _outproj_kernel(l_ref, x_ref, yf_ref, yb_ref, xa_ref, zs_ref, zg_ref, of_ref, ob_ref,
                    dexp_ref, sg_ref, gg_ref, w_ref, post_ref, o_ref):
    gw = SSD_WIDTH // SSD_GROUPS
    hq = gw // LANES
    acc = None
    for g in range(SSD_GROUPS):
        sl = slice(g * gw, (g + 1) * gw)
        xg = jnp.concatenate([xa_ref[hq * g + p] for p in range(hq)], axis=1).astype(F32)
        y = yf_ref[:, sl].astype(F32) + yb_ref[:, sl].astype(F32) + dexp_ref[:, sl] * xg
        y = y * _silu(zs_ref[:, sl].astype(F32))
        part = _mm(_rms(y, sg_ref[:, sl]).astype(BF16), w_ref[sl, :])
        acc = part if acc is None else acc + part
    for g in range(GDN_WIDTH // gw):
        os = []
        for h in range(hq * g, hq * (g + 1)):
            o = of_ref[h].astype(F32) + ob_ref[h].astype(F32)
            o = _rms(o, gg_ref[...]) * _silu(zg_ref[:, h * LANES:(h + 1) * LANES].astype(F32))
            os.append(o.astype(BF16))
        acc = acc + _mm(jnp.concatenate(os, axis=1), w_ref[SSD_WIDTH + g * gw:SSD_WIDTH + (g + 1) * gw, :])
    o_ref[...] = x_ref[...] + _rms(acc, post_ref[...])


def _outproj_call(layer, x, yf, yb, act, z, of, ob, dexp, ssd_g, gdn_g, w_out, post_g, *, tm=256):
    seq, d = x.shape
    row_d = pl.BlockSpec((None, 1, d), lambda i, l: (l[0], 0, 0))
    gs = pltpu.PrefetchScalarGridSpec(
        num_scalar_prefetch=1,
        grid=(seq // tm,),
        in_specs=[
            pl.BlockSpec((tm, d), lambda i, l: (i, 0)),
            pl.BlockSpec((tm, SSD_WIDTH), lambda i, l: (i, 0)),
            pl.BlockSpec((tm, SSD_WIDTH), lambda i, l: (i, 0)),
            pl.BlockSpec((16, tm, LANES), lambda i, l: (0, i, 0)),
            pl.BlockSpec((tm, SSD_WIDTH), lambda i, l: (i, COL_ZS // SSD_WIDTH)),
            pl.BlockSpec((tm, GDN_WIDTH), lambda i, l: (i, COL_ZG // GDN_WIDTH)),
            pl.BlockSpec((GDN_HEADS, tm, LANES), lambda i, l: (0, i, 0)),
            pl.BlockSpec((GDN_HEADS, tm, LANES), lambda i, l: (0, i, 0)),
            row_d, row_d,
            pl.BlockSpec((None, 1, LANES), lambda i, l: (l[0], 0, 0)),
            pl.BlockSpec((None, 2 * d, d), lambda i, l: (l[0], 0, 0), pipeline_mode=pl.Buffered(1)),
            row_d,
        ],
        out_specs=pl.BlockSpec((tm, d), lambda i, l: (i, 0)),
    )
    return pl.pallas_call(
        _outproj_kernel,
        grid_spec=gs,
        out_shape=jax.ShapeDtypeStruct((seq, d), F32),
        compiler_params=_params(("parallel",)),
        name="outproj",
    )(layer, x, yf, yb, act, z, z, of, ob, dexp, ssd_g, gdn_g, w_out, post_g)


def kernel(x, ffn1_pre_g, ffn1_w_gu, ffn1_w_down, ffn1_post_g, mix_pre_g, w_in, ssd_conv_w, ssd_conv_b,
           ssd_dt_bias, ssd_a_log, ssd_d, ssd_norm_g, gdn_conv_w, gdn_dt_bias, gdn_a_log, gdn_norm_g, w_out,
           mix_post_g, ffn2_pre_g, ffn2_w_gu, ffn2_w_down, ffn2_post_g):
    row = lambda g: g.astype(F32).reshape(DEPTH, 1, -1)

    s0, s1, s2 = SSD_WIDTH, SSD_WIDTH + SSD_CONV_CH, SSD_WIDTH + SSD_CONV_CH + 2 * SSD_HEADS
    g1, g2 = s2 + GDN_CONV_CH, s2 + GDN_CONV_CH + GDN_WIDTH
    w_in16 = lax.optimization_barrier(w_in.astype(BF16))
    w_in_p = jnp.concatenate([w_in16[..., a:b]
                              for a, b in ((0, s0), (g1, g2), (s0, s1), (s2, g1), (s1, s2), (g2, D_IN_PROJ))], axis=-1)
    conv_w = jnp.concatenate([ssd_conv_w, gdn_conv_w], axis=-1).astype(F32)
    conv_b = jnp.concatenate([ssd_conv_b.astype(F32), jnp.zeros((DEPTH, GDN_CONV_CH), F32)], axis=-1)[:, None, :]
    pad = jnp.zeros((DEPTH, LANES - G_BETA), F32)
    bias = jnp.concatenate([ssd_dt_bias.reshape(DEPTH, -1), gdn_dt_bias.reshape(DEPTH, -1), pad], axis=-1)
    alog = jnp.concatenate([ssd_a_log.reshape(DEPTH, -1), gdn_a_log.reshape(DEPTH, -1), pad], axis=-1)
    brow, arow = bias[:, None, :], alog[:, None, :]
    bcol, acol = bias[:, :, None], alog[:, :, None]
    dexp = jnp.repeat(ssd_d.astype(F32), SSD_HEAD_DIM, axis=-1)[:, None, :]
    f1_gu, f1_dn = ffn1_w_gu.astype(BF16), ffn1_w_down.astype(BF16)
    f2_gu, f2_dn = ffn2_w_gu.astype(BF16), ffn2_w_down.astype(BF16)
    w_out16 = w_out.astype(BF16)
    f1_pre, f1_post, f2_pre, f2_post = row(ffn1_pre_g), row(ffn1_post_g), row(ffn2_pre_g), row(ffn2_post_g)
    m_pre, m_post, ssd_g, gdn_g = row(mix_pre_g), row(mix_post_g), row(ssd_norm_g), row(gdn_norm_g)

    def layer_fn(i, xc):
        layer = jnp.reshape(i, (1,)).astype(jnp.int32)
        xc = _ffn_call(layer, xc, f1_pre, f1_gu, f1_dn, f1_post)
        z, c1, c2, c3, r2, act = _inproj_call(layer, xc, m_pre, w_in_p, conv_w, conv_b, brow, arow, bcol, acol)
        yf, yb = _ssd_call(layer, act, c1, c2, c3, r2)
        of, ob = _gdn_call(layer, act, c1, c2, c3, r2)
        xc = _outproj_call(layer, xc, yf, yb, act, z, of, ob, dexp, ssd_g, gdn_g, w_out16, m_post)
        xc = _ffn_call(layer, xc, f2_pre, f2_gu, f2_dn, f2_post)
        return xc

    out = lax.fori_loop(0, DEPTH, layer_fn, x.reshape(SEQ, D_MODEL).astype(F32))
    return out.reshape(x.shape)
```

```python
import functools

import jax
import jax.numpy as jnp
from jax import lax
from jax.experimental import pallas as pl
from jax.experimental.pallas import tpu as pltpu

F32 = jnp.float32
BF16 = jnp.bfloat16
HIGHEST = lax.Precision.HIGHEST

D_MODEL = 2048
SEQ = 8192
DEPTH = 4
EPS = 1e-6
D_FF = 5632
CONV_WIDTH = 5
SSD_HEADS = 32
SSD_HEAD_DIM = 64
SSD_WIDTH = 2048
SSD_GROUPS = 4
SSD_STATE = 128
SSD_CHUNK = 128
SSD_BC = 512
SSD_CONV_CH = SSD_WIDTH + 2 * SSD_BC
GDN_HEADS = 16
GDN_HEAD_DIM = 128
GDN_WIDTH = 2048
GDN_CHUNK = 64
GDN_CONV_CH = 3 * GDN_WIDTH
D_IN_PROJ = 13440

LANES = 128
COL_ZS = 0
COL_ZG = 2048
COL_CONV = 4096
N_CONV = SSD_CONV_CH + GDN_CONV_CH
COL_GATE = COL_CONV + N_CONV
N_CONV_BLK = N_CONV // LANES
G_SSD = 0
G_GDN = 64
G_BETA = 96

GDN_SOLVE_CHUNK = 128
GDN_INV_SHIFT = 6
GDN_BLOCK = 512
assert 2 << GDN_INV_SHIFT == GDN_SOLVE_CHUNK
assert GDN_SOLVE_CHUNK == SSD_CHUNK
VMEM_LIMIT = 56 * 1024 * 1024


def _mm(a, b, precision=None):
    return jnp.dot(a, b, preferred_element_type=F32, precision=precision)


def _nt(a, b, precision=None):
    return lax.dot_general(a, b, (((1,), (1,)), ((), ())), preferred_element_type=F32, precision=precision)


def _tn(a, b):
    return lax.dot_general(a, b, (((0,), (0,)), ((), ())), preferred_element_type=F32)


def _rms(x, g):
    return x * lax.rsqrt(jnp.mean(x * x, axis=-1, keepdims=True) + EPS) * g


def _silu(x):
    return x * jax.nn.sigmoid(x)


def _softplus(x):
    return jnp.maximum(x, 0.0) + jnp.log1p(jnp.exp(-jnp.abs(x)))


def _params(sem):
    return pltpu.CompilerParams(dimension_semantics=sem, vmem_limit_bytes=VMEM_LIMIT)


def _ffn_kernel(l_ref, x_ref, pre_ref, wg_ref, wu_ref, wd_ref, post_ref, o_ref, h_ref, acc_ref, *, nj):
    j = pl.program_id(1)

    @pl.when(j == 0)
    def _():
        h_ref[...] = _rms(x_ref[...], pre_ref[...]).astype(BF16)
        acc_ref[...] = jnp.zeros_like(acc_ref)

    h = h_ref[...]
    gate = _mm(h, wg_ref[...])
    up = _mm(h, wu_ref[...])
    act = (_silu(gate) * up).astype(BF16)
    acc_ref[...] += _mm(act, wd_ref[...])

    @pl.when(j == nj - 1)
    def _():
        o_ref[...] = x_ref[...] + 0.5 * _rms(acc_ref[...], post_ref[...])


def _ffn_call(layer, x, pre_g, w_gu, w_down, post_g, *, tm=512, tf=512):
    seq, d = x.shape
    nj = D_FF // tf
    gs = pltpu.PrefetchScalarGridSpec(
        num_scalar_prefetch=1,
        grid=(seq // tm, nj),
        in_specs=[
            pl.BlockSpec((tm, d), lambda i, j, l: (i, 0)),
            pl.BlockSpec((None, 1, d), lambda i, j, l: (l[0], 0, 0)),
            pl.BlockSpec((None, d, tf), lambda i, j, l: (l[0], 0, j)),
            pl.BlockSpec((None, d, tf), lambda i, j, l: (l[0], 0, j + nj)),
            pl.BlockSpec((None, tf, d), lambda i, j, l: (l[0], j, 0)),
            pl.BlockSpec((None, 1, d), lambda i, j, l: (l[0], 0, 0)),
        ],
        out_specs=pl.BlockSpec((tm, d), lambda i, j, l: (i, 0)),
        scratch_shapes=[pltpu.VMEM((tm, d), BF16), pltpu.VMEM((tm, d), F32)],
    )
    return pl.pallas_call(
        functools.partial(_ffn_kernel, nj=nj),
        grid_spec=gs,
        out_shape=jax.ShapeDtypeStruct((seq, d), F32),
        compiler_params=_params(("parallel", "arbitrary")),
        name="ffn",
    )(layer, x, pre_g, w_gu, w_gu, w_down, post_g)


HALO = 16
CONV_SUB = 256


def _gate_block(raw, brow, arow, bcol, acol):
    q = SSD_CHUNK
    lane = lax.broadcasted_iota(jnp.int32, (q, LANES), 1)
    sp = _softplus(raw + brow)
    c1 = jnp.where(lane < G_BETA, sp, jax.nn.sigmoid(raw))
    a = jnp.where(lane < G_BETA, sp * (-jnp.exp(arow)), 0.0)

    ri = lax.broadcasted_iota(jnp.int32, (q, q), 0)
    ci = lax.broadcasted_iota(jnp.int32, (q, q), 1)
    lower = (ri >= ci).astype(F32)
    upper = (ri <= ci).astype(F32)
    ones = jnp.ones((q, q), F32)

    def is_fwd(idx):
        return (idx < G_SSD + SSD_HEADS) | ((idx >= G_GDN) & (idx < G_GDN + GDN_HEADS))

    c2 = jnp.where(is_fwd(lane), _mm(lower, a, HIGHEST), _mm(upper, a, HIGHEST))
    c3 = _mm(ones, a, HIGHEST)
    sub = lax.broadcasted_iota(jnp.int32, (LANES, q), 0)
    at = jnp.where(sub < G_BETA, _softplus(raw.T + bcol) * (-jnp.exp(acol)), 0.0)
    r2 = jnp.where(is_fwd(sub), _nt(at, lower, HIGHEST), _nt(at, upper, HIGHEST))
    return c1, c2, c3, r2


def _inproj_kernel(l_ref, xp_ref, x_ref, xn_ref, pre_ref, w_ref, wg_ref, cw_ref, cb_ref,
                   brow_ref, arow_ref, bcol_ref, acol_ref,
                   z_ref, c1_ref, c2_ref, c3_ref, r2_ref, act_ref, h_ref, *, tm, ni, nz):
    i = pl.program_id(0)
    j = pl.program_id(1)

    @pl.when(j == 0)
    def _():
        pre = pre_ref[...]
        h_ref[0:HALO, :] = jnp.where(i > 0, _rms(xp_ref[...], pre), 0.0).astype(BF16)
        h_ref[HALO:HALO + tm, :] = _rms(x_ref[...], pre).astype(BF16)
        h_ref[HALO + tm:, :] = jnp.where(i < ni - 1, _rms(xn_ref[...], pre), 0.0).astype(BF16)
        gate = _mm(h_ref[HALO:HALO + tm, :], wg_ref[...])
        for c in range(tm // SSD_CHUNK):
            rows = slice(c * SSD_CHUNK, (c + 1) * SSD_CHUNK)
            c1, c2, c3, r2 = _gate_block(gate[rows], brow_ref[...], arow_ref[...], bcol_ref[...], acol_ref[...])
            c1_ref[rows, :] = c1
            c2_ref[rows, :] = c2
            c3_ref[rows, :] = c3
            r2_ref[:, rows] = r2

    @pl.when(j < nz)
    def _():
        z_ref[...] = _mm(h_ref[HALO:HALO + tm, :], w_ref[...]).astype(BF16)

    @pl.when(j >= nz)
    def _():
        pad = CONV_WIDTH // 2
        h = h_ref[...]
        for s in range(w_ref.shape[1] // CONV_SUB):
            cols = slice(s * CONV_SUB, (s + 1) * CONV_SUB)
            acc = _mm(h, w_ref[:, cols])
            y = cb_ref[:, cols]
            for t in range(CONV_WIDTH):
                tap = acc if t == pad else pltpu.roll(acc, (pad - t) % acc.shape[0], 0)
                y = y + cw_ref[t:t + 1, cols] * tap[HALO:HALO + tm, :]
            y = _silu(y).astype(BF16)
            for b in range(CONV_SUB // LANES):
                act_ref[s * (CONV_SUB // LANES) + b] = y[:, b * LANES:(b + 1) * LANES]


def _inproj_call(layer, x, pre_g, w_in, conv_w, conv_b, brow, arow, bcol, acol, *, tm=1024, tn=1024):
    seq, d = x.shape
    ni = seq // tm
    nz = COL_CONV // tn
    nj = COL_GATE // tn
    hb = tm // HALO
    row_p = pl.BlockSpec((None, 1, LANES), lambda i, j, l: (l[0], 0, 0))
    col_p = pl.BlockSpec((None, LANES, 1), lambda i, j, l: (l[0], 0, 0))
    colform = pl.BlockSpec((tm, LANES), lambda i, j, l: (i, 0))
    col_shape = jax.ShapeDtypeStruct((seq, LANES), F32)
    gs = pltpu.PrefetchScalarGridSpec(
        num_scalar_prefetch=1,
        grid=(ni, nj),
        in_specs=[
            pl.BlockSpec((HALO, d), lambda i, j, l: (jnp.maximum(i * hb - 1, 0), 0)),
            pl.BlockSpec((tm, d), lambda i, j, l: (i, 0)),
            pl.BlockSpec((HALO, d), lambda i, j, l: (jnp.minimum((i + 1) * hb, seq // HALO - 1), 0)),
            pl.BlockSpec((None, 1, d), lambda i, j, l: (l[0], 0, 0)),
            pl.BlockSpec((None, d, tn), lambda i, j, l: (l[0], 0, j)),
            pl.BlockSpec((None, d, LANES), lambda i, j, l: (l[0], 0, COL_GATE // LANES)),
            pl.BlockSpec((None, CONV_WIDTH, tn), lambda i, j, l: (l[0], 0, jnp.maximum(j - nz, 0))),
            pl.BlockSpec((None, 1, tn), lambda i, j, l: (l[0], 0, jnp.maximum(j - nz, 0))),
            row_p, row_p, col_p, col_p,
        ],
        out_specs=[
            pl.BlockSpec((tm, tn), lambda i, j, l: (i, jnp.minimum(j, nz - 1))),
            colform, colform, colform,
            pl.BlockSpec((LANES, tm), lambda i, j, l: (0, i)),
            pl.BlockSpec((tn // LANES, tm, LANES), lambda i, j, l: (jnp.maximum(j - nz, 0), i, 0)),
        ],
        scratch_shapes=[pltpu.VMEM((tm + 2 * HALO, d), BF16)],
    )
    return pl.pallas_call(
        functools.partial(_inproj_kernel, tm=tm, ni=ni, nz=nz),
        grid_spec=gs,
        out_shape=[jax.ShapeDtypeStruct((seq, COL_CONV), BF16),
                   col_shape, col_shape, col_shape, jax.ShapeDtypeStruct((LANES, seq), F32),
                   jax.ShapeDtypeStruct((N_CONV_BLK, seq, LANES), BF16)],
        compiler_params=_params(("parallel", "arbitrary")),
        name="inproj",
    )(layer, x, x, x, pre_g, w_in, w_in, conv_w, conv_b, brow, arow, bcol, acol)


def _ssd_dir(x_ref, b_ref, c_ref, c1_ref, c2_ref, c3_ref, r2_ref, s_ref, y_ref, *, d):
    q = SSD_CHUNK
    gw = SSD_WIDTH // SSD_GROUPS
    hpg = SSD_HEADS // SSD_GROUPS
    dt_all = c1_ref[...]
    cs_all = c2_ref[...]
    tot_all = c3_ref[...]
    etot = jnp.exp(tot_all[0:16])
    etot_hi = etot.astype(BF16)
    etot_lo = (etot - etot_hi.astype(F32)).astype(BF16)
    stacked = jnp.concatenate([dt_all.astype(BF16), jnp.exp(cs_all).astype(BF16),
                               jnp.exp(tot_all - cs_all).astype(BF16), etot_hi, etot_lo], axis=0)
    ri = lax.broadcasted_iota(jnp.int32, (q, q), 0)
    ci = lax.broadcasted_iota(jnp.int32, (q, q), 1)
    mask = (ri >= ci) if d == 0 else (ri <= ci)
    kk = lax.broadcasted_iota(jnp.int32, (LANES, gw), 0)
    nn = lax.broadcasted_iota(jnp.int32, (LANES, gw), 1)
    lane = lax.broadcasted_iota(jnp.int32, (q, LANES), 1)
    for g in range(SSD_GROUPS):
        base = G_SSD + SSD_HEADS * d + hpg * g
        sel = (kk == base + (nn >> 6)).astype(BF16)
        ex = _mm(stacked, sel)
        dt_e, ecs_e, dte_e = ex[0:q], ex[q:2 * q], ex[2 * q:3 * q]
        etot_e = ex[3 * q:3 * q + 1] + ex[3 * q + 16:3 * q + 17]
        xg = jnp.concatenate([x_ref[4 * g + p] for p in range(4)], axis=1).astype(F32)
        xdt = xg * dt_e
        xdt16 = xdt.astype(BF16)
        xdte16 = (xdt * dte_e).astype(BF16)
        bg = b_ref[g]
        cg = c_ref[g]
        cb = _nt(cg, bg)
        s_prev = s_ref[d, g]
        y_off = _mm(cg, s_prev.astype(BF16)) * ecs_e
        s_ref[d, g] = s_prev * etot_e + _tn(bg, xdte16)
        ys = []
        for pair in range(hpg // 2):
            ms = []
            for hh in (2 * pair, 2 * pair + 1):
                r = base + hh
                dec = jnp.exp(jnp.where(mask, cs_all[:, r:r + 1] - r2_ref[r:r + 1, :], -jnp.inf))
                ms.append((cb * dec).astype(BF16))
            lhs = jnp.concatenate(ms, axis=1)
            xp = xdt16[:, pair * LANES:(pair + 1) * LANES]
            zero = jnp.zeros_like(xp)
            rhs = jnp.concatenate([jnp.where(lane < SSD_HEAD_DIM, xp, zero),
                                   jnp.where(lane >= SSD_HEAD_DIM, xp, zero)], axis=0)
            ys.append(_mm(lhs, rhs))
        y = jnp.concatenate(ys, axis=1) + y_off
        y_ref[:, g * gw:(g + 1) * gw] = y.astype(BF16)


def _ssd_kernel(l_ref,
                xf_ref, bf_ref, cf_ref, c1f_ref, c2f_ref, c3f_ref, r2f_ref,
                xb_ref, bb_ref, cb_ref, c1b_ref, c2b_ref, c3b_ref, r2b_ref,
                yf_ref, yb_ref, s_ref):
    @pl.when(pl.program_id(0) == 0)
    def _():
        s_ref[...] = jnp.zeros_like(s_ref)

    _ssd_dir(xf_ref, bf_ref, cf_ref, c1f_ref, c2f_ref, c3f_ref, r2f_ref, s_ref, yf_ref, d=0)
    _ssd_dir(xb_ref, bb_ref, cb_ref, c1b_ref, c2b_ref, c3b_ref, r2b_ref, s_ref, yb_ref, d=1)


def _ssd_call(layer, act, c1, c2, c3, r2):
    seq = c1.shape[0]
    q = SSD_CHUNK
    nc = seq // q

    def dir_specs(rowblk):
        return [
            pl.BlockSpec((16, q, LANES), lambda i, l: (0, rowblk(i), 0)),
            pl.BlockSpec((4, q, LANES), lambda i, l: (4, rowblk(i), 0)),
            pl.BlockSpec((4, q, LANES), lambda i, l: (5, rowblk(i), 0)),
            pl.BlockSpec((q, LANES), lambda i, l: (rowblk(i), 0)),
            pl.BlockSpec((q, LANES), lambda i, l: (rowblk(i), 0)),
            pl.BlockSpec((q, LANES), lambda i, l: (rowblk(i), 0)),
            pl.BlockSpec((LANES, q), lambda i, l: (0, rowblk(i))),
        ]

    fwd = lambda i: i
    bwd = lambda i: nc - 1 - i
    gs = pltpu.PrefetchScalarGridSpec(
        num_scalar_prefetch=1,
        grid=(nc,),
        in_specs=dir_specs(fwd) + dir_specs(bwd),
        out_specs=[pl.BlockSpec((q, SSD_WIDTH), lambda i, l: (fwd(i), 0)),
                   pl.BlockSpec((q, SSD_WIDTH), lambda i, l: (bwd(i), 0))],
        scratch_shapes=[pltpu.VMEM((2, SSD_GROUPS, SSD_STATE, SSD_WIDTH // SSD_GROUPS), F32)],
    )
    y_shape = jax.ShapeDtypeStruct((seq, SSD_WIDTH), BF16)
    args = (act, act, act, c1, c2, c3, r2)
    return pl.pallas_call(
        _ssd_kernel,
        grid_spec=gs,
        out_shape=[y_shape, y_shape],
        compiler_params=_params(("arbitrary",)),
        name="ssd",
    )(layer, *args, *args)


def _gdn_prep(q_ref, k_ref, v_ref, c1_ref, c2_ref, c3_ref, r2_ref, *, j, d, head, rows):
    c = GDN_SOLVE_CHUNK
    dh = GDN_HEAD_DIM
    q = q_ref[j, rows, :].astype(F32)
    k = k_ref[j, rows, :].astype(F32)
    v = v_ref[j, rows, :].astype(F32)
    q = q * (lax.rsqrt(jnp.sum(q * q, axis=-1, keepdims=True) + EPS) * (dh ** -0.5))
    k = k * lax.rsqrt(jnp.sum(k * k, axis=-1, keepdims=True) + EPS)

    lane = lax.broadcasted_iota(jnp.int32, (c, LANES), 1)
    ig = G_GDN + GDN_HEADS * d + head
    ib = G_BETA + GDN_HEADS * d + head
    gcs = jnp.sum(jnp.where(lane == ig, c2_ref[rows, :], 0.0), axis=-1, keepdims=True)
    gtot = jnp.sum(jnp.where(lane == ig, c3_ref[rows, :], 0.0), axis=-1, keepdims=True)
    beta = jnp.sum(jnp.where(lane == ib, c1_ref[rows, :], 0.0), axis=-1, keepdims=True)
    grow = r2_ref[pl.ds(ig, 1), :][:, rows]

    ri = lax.broadcasted_iota(jnp.int32, (c, c), 0)
    ci = lax.broadcasted_iota(jnp.int32, (c, c), 1)
    incl = (ri >= ci) if d == 0 else (ri <= ci)
    strict = (ri > ci) if d == 0 else (ri < ci)
    decay = jnp.exp(jnp.where(incl, gcs - grow, -jnp.inf))

    kb = k * beta
    eg = jnp.exp(gcs)
    kq = _nt(jnp.concatenate([kb, q], axis=0).astype(BF16), k.astype(BF16))
    p = jnp.where(strict, -(kq[:c] * decay), 0.0)
    half = c // 2
    left = lax.broadcasted_iota(jnp.int32, (half, LANES), 1) < half
    return dict(
        q=jnp.where(left, p[:half], p[half:]),
        poff16=jnp.where(left, p[half:], p[:half]).astype(BF16),
        attn16=(kq[c:] * decay).astype(BF16),
        rhs16=jnp.concatenate([v * beta, kb * eg], axis=1).astype(BF16),
        qeg=q * eg,
        ket16=(k * jnp.exp(gtot - gcs)).T.astype(BF16),
        ld=jnp.exp(gtot[0:1, :]),
    )


def _gdn_kernel(l_ref,
                qf_ref, kf_ref, vf_ref, c1f_ref, c2f_ref, c3f_ref, r2f_ref,
                qb_ref, kb_ref, vb_ref, c1b_ref, c2b_ref, c3b_ref, r2b_ref,
                of_ref, ob_ref, s_ref, *, hb):
    @pl.when(pl.program_id(1) == 0)
    def _():
        s_ref[...] = jnp.zeros_like(s_ref)

    c = GDN_SOLVE_CHUNK
    dh = GDN_HEAD_DIM
    ncb = GDN_BLOCK // c
    in_refs = ((qf_ref, kf_ref, vf_ref, c1f_ref, c2f_ref, c3f_ref, r2f_ref),
               (qb_ref, kb_ref, vb_ref, c1b_ref, c2b_ref, c3b_ref, r2b_ref))
    out_refs = (of_ref, ob_ref)
    streams = [(j, d) for j in range(hb) for d in range(2)]
    state = {sd: s_ref[sd[1], sd[0]] for sd in streams}

    half = c // 2
    lane_h = lax.broadcasted_iota(jnp.int32, (half, LANES), 1)
    row_h = lax.broadcasted_iota(jnp.int32, (half, LANES), 0)
    left = lane_h < half
    eye2 = (lane_h == row_h) | (lane_h == row_h + half)

    def blockdiag16(x):
        return jnp.concatenate([jnp.where(left, x, 0.0), jnp.where(left, 0.0, x)], axis=0).astype(BF16)

    def scan_read(units):
        for (j, d), u in units.items():
            ws = _mm(u["lhs16"], state[(j, d)].astype(BF16))
            u["v_new16"] = (u["u"] - ws[:c]).astype(BF16)
            out_refs[d][j, u["rows"], :] = (ws[c:] + u["au"]).astype(BF16)

    def scan_write(units):
        for sd, u in units.items():
            state[sd] = state[sd] * u["ld"] + _mm(u["ket16"], u["v_new16"])

    prev = None
    for phase in range(ncb):
        units = {}
        for (j, d) in streams:
            cc = phase if d == 0 else ncb - 1 - phase
            rows = slice(cc * c, (cc + 1) * c)
            units[(j, d)] = _gdn_prep(*in_refs[d], j=j, d=d, head=pl.program_id(0) * hb + j, rows=rows)
            units[(j, d)]["rows"] = rows
        woven = [] if prev is None else [functools.partial(scan_read, prev), functools.partial(scan_write, prev)]
        for u in units.values():
            u["t"] = jnp.where(eye2, 1.0, u["q"])
            u["q"] = _mm(u["q"].astype(BF16), blockdiag16(u["q"]))
        if woven:
            woven[0]()
        for level in range(1, GDN_INV_SHIFT):
            for u in units.values():
                rhs = blockdiag16(u["q"])
                if level == GDN_INV_SHIFT - 1:
                    u["t"] = u["t"] + _mm(u["t"].astype(BF16), rhs)
                else:
                    qt = _mm(jnp.concatenate([u["q"], u["t"]], axis=0).astype(BF16), rhs)
                    u["q"] = qt[:half]
                    u["t"] = u["t"] + qt[half:]
            if level < len(woven):
                woven[level]()
        for u in units.values():
            u["tbd16"] = blockdiag16(u["t"])
            u["y16"] = _mm(u["poff16"], u["tbd16"]).astype(BF16)
        zeros = jnp.zeros((half, LANES), BF16)
        for (j, d), u in units.items():
            tbd16 = u["tbd16"]
            if d == 0:
                z = _mm(tbd16[half:], jnp.concatenate([zeros, u["y16"]], axis=0))
                u["t16"] = jnp.concatenate([tbd16[:half], jnp.where(left, z, u["t"]).astype(BF16)], axis=0)
            else:
                z = _mm(tbd16[:half], jnp.concatenate([u["y16"], zeros], axis=0))
                u["t16"] = jnp.concatenate([jnp.where(left, u["t"], z).astype(BF16), tbd16[half:]], axis=0)
        for u in units.values():
            u["sol"] = _mm(u["t16"], u["rhs16"])
        for u in units.values():
            aw = _mm(u["attn16"], u["sol"].astype(BF16))
            u["u"] = u["sol"][:, :dh]
            u["au"] = aw[:, :dh]
            u["lhs16"] = jnp.concatenate([u["sol"][:, dh:], u["qeg"] - aw[:, dh:]], axis=0).astype(BF16)
        prev = units
    scan_read(prev)
    scan_write(prev)
    for (j, d) in streams:
        s_ref[d, j] = state[(j, d)]


def _gdn_call(layer, act, c1, c2, c3, r2, *, hb=8):
    seq = c1.shape[0]
    r = GDN_BLOCK
    nb = seq // r
    q0, k0, v0 = 24 // hb, 40 // hb, 56 // hb

    def dir_specs(rowblk):
        return [
            pl.BlockSpec((hb, r, LANES), lambda h, i, l: (q0 + h, rowblk(i), 0)),
            pl.BlockSpec((hb, r, LANES), lambda h, i, l: (k0 + h, rowblk(i), 0)),
            pl.BlockSpec((hb, r, LANES), lambda h, i, l: (v0 + h, rowblk(i), 0)),
            pl.BlockSpec((r, LANES), lambda h, i, l: (rowblk(i), 0)),
            pl.BlockSpec((r, LANES), lambda h, i, l: (rowblk(i), 0)),
            pl.BlockSpec((r, LANES), lambda h, i, l: (rowblk(i), 0)),
            pl.BlockSpec((LANES, r), lambda h, i, l: (0, rowblk(i))),
        ]

    fwd = lambda i: i
    bwd = lambda i: nb - 1 - i
    gs = pltpu.PrefetchScalarGridSpec(
        num_scalar_prefetch=1,
        grid=(GDN_HEADS // hb, nb),
        in_specs=dir_specs(fwd) + dir_specs(bwd),
        out_specs=[pl.BlockSpec((hb, r, LANES), lambda h, i, l: (h, fwd(i), 0)),
                   pl.BlockSpec((hb, r, LANES), lambda h, i, l: (h, bwd(i), 0))],
        scratch_shapes=[pltpu.VMEM((2, hb, GDN_HEAD_DIM, GDN_HEAD_DIM), F32)],
    )
    o_shape = jax.ShapeDtypeStruct((GDN_HEADS, seq, LANES), BF16)
    args = (act, act, act, c1, c2, c3, r2)
    return pl.pallas_call(
        functools.partial(_gdn_kernel, hb=hb),
        grid_spec=gs,
        out_shape=[o_shape, o_shape],
        compiler_params=_params(("parallel", "arbitrary")),
        name="gdn",
    )(layer, *args, *args)


def _outproj_kernel(l_ref, x_ref, yf_ref, yb_ref, xa_ref, zs_ref, zg_ref, of_ref, ob_ref,
                    dexp_ref, sg_ref, gg_ref, w_ref, post_ref, o_ref):
    gw = SSD_WIDTH // SSD_GROUPS
    hq = gw // LANES
    acc = None
    for g in range(SSD_GROUPS):
        sl = slice(g * gw, (g + 1) * gw)
        xg = jnp.concatenate([xa_ref[hq * g + p] for p in range(hq)], axis=1).astype(F32)
        y = yf_ref[:, sl].astype(F32) + yb_ref[:, sl].astype(F32) + dexp_ref[:, sl] * xg
        y = y * _silu(zs_ref[:, sl].astype(F32))
        part = _mm(_rms(y, sg_ref[:, sl]).astype(BF16), w_ref[sl, :])
        acc = part if acc is None else acc + part
    for g in range(GDN_WIDTH // gw):
        os = []
        for h in range(hq * g, hq * (g + 1)):
            o = of_ref[h].astype(F32) + ob_ref[h].astype(F32)
            o = _rms(o, gg_ref[...]) * _silu(zg_ref[:, h * LANES:(h + 1) * LANES].astype(F32))
            os.append(o.astype(BF16))
        acc = acc + _mm(jnp.concatenate(os, axis=1), w_ref[SSD_WIDTH + g * gw:SSD_WIDTH + (g + 1) * gw, :])
    o_ref[...] = x_ref[...] + _rms(acc, post_ref[...])


def _outproj_call(layer, x, yf, yb, act, z, of, ob, dexp, ssd_g, gdn_g, w_out, post_g, *, tm=256):
    seq, d = x.shape
    row_d = pl.BlockSpec((None, 1, d), lambda i, l: (l[0], 0, 0))
    gs = pltpu.PrefetchScalarGridSpec(
        num_scalar_prefetch=1,
        grid=(seq // tm,),
        in_specs=[
            pl.BlockSpec((tm, d), lambda i, l: (i, 0)),
            pl.BlockSpec((tm, SSD_WIDTH), lambda i, l: (i, 0)),
            pl.BlockSpec((tm, SSD_WIDTH), lambda i, l: (i, 0)),
            pl.BlockSpec((16, tm, LANES), lambda i, l: (0, i, 0)),
            pl.BlockSpec((tm, SSD_WIDTH), lambda i, l: (i, COL_ZS // SSD_WIDTH)),
            pl.BlockSpec((tm, GDN_WIDTH), lambda i, l: (i, COL_ZG // GDN_WIDTH)),
            pl.BlockSpec((GDN_HEADS, tm, LANES), lambda i, l: (0, i, 0)),
            pl.BlockSpec((GDN_HEADS, tm, LANES), lambda i, l: (0, i, 0)),
            row_d, row_d,
            pl.BlockSpec((None, 1, LANES), lambda i, l: (l[0], 0, 0)),
            pl.BlockSpec((None, 2 * d, d), lambda i, l: (l[0], 0, 0), pipeline_mode=pl.Buffered(1)),
            row_d,
        ],
        out_specs=pl.BlockSpec((tm, d), lambda i, l: (i, 0)),
    )
    return pl.pallas_call(
        _outproj_kernel,
        grid_spec=gs,
        out_shape=jax.ShapeDtypeStruct((seq, d), F32),
        compiler_params=_params(("parallel",)),
        name="outproj",
    )(layer, x, yf, yb, act, z, z, of, ob, dexp, ssd_g, gdn_g, w_out, post_g)


def kernel(x, ffn1_pre_g, ffn1_w_gu, ffn1_w_down, ffn1_post_g, mix_pre_g, w_in, ssd_conv_w, ssd_conv_b,
           ssd_dt_bias, ssd_a_log, ssd_d, ssd_norm_g, gdn_conv_w, gdn_dt_bias, gdn_a_log, gdn_norm_g, w_out,
           mix_post_g, ffn2_pre_g, ffn2_w_gu, ffn2_w_down, ffn2_post_g):
    row = lambda g: g.astype(F32).reshape(DEPTH, 1, -1)

    s0, s1, s2 = SSD_WIDTH, SSD_WIDTH + SSD_CONV_CH, SSD_WIDTH + SSD_CONV_CH + 2 * SSD_HEADS
    g1, g2 = s2 + GDN_CONV_CH, s2 + GDN_CONV_CH + GDN_WIDTH
    w_in16 = lax.optimization_barrier(w_in.astype(BF16))
    w_in_p = jnp.concatenate([w_in16[..., a:b]
                              for a, b in ((0, s0), (g1, g2), (s0, s1), (s2, g1), (s1, s2), (g2, D_IN_PROJ))], axis=-1)
    conv_w = jnp.concatenate([ssd_conv_w, gdn_conv_w], axis=-1).astype(F32)
    conv_b = jnp.concatenate([ssd_conv_b.astype(F32), jnp.zeros((DEPTH, GDN_CONV_CH), F32)], axis=-1)[:, None, :]
    pad = jnp.zeros((DEPTH, LANES - G_BETA), F32)
    bias = jnp.concatenate([ssd_dt_bias.reshape(DEPTH, -1), gdn_dt_bias.reshape(DEPTH, -1), pad], axis=-1)
    alog = jnp.concatenate([ssd_a_log.reshape(DEPTH, -1), gdn_a_log.reshape(DEPTH, -1), pad], axis=-1)
    brow, arow = bias[:, None, :], alog[:, None, :]
    bcol, acol = bias[:, :, None], alog[:, :, None]
    dexp = jnp.repeat(ssd_d.astype(F32), SSD_HEAD_DIM, axis=-1)[:, None, :]
    f1_gu, f1_dn = ffn1_w_gu.astype(BF16), ffn1_w_down.astype(BF16)
    f2_gu, f2_dn = ffn2_w_gu.astype(BF16), ffn2_w_down.astype(BF16)
    w_out16 = w_out.astype(BF16)
    f1_pre, f1_post, f2_pre, f2_post = row(ffn1_pre_g), row(ffn1_post_g), row(ffn2_pre_g), row(ffn2_post_g)
    m_pre, m_post, ssd_g, gdn_g = row(mix_pre_g), row(mix_post_g), row(ssd_norm_g), row(gdn_norm_g)

    def layer_fn(i, xc):
        layer = jnp.reshape(i, (1,)).astype(jnp.int32)
        xc = _ffn_call(layer, xc, f1_pre, f1_gu, f1_dn, f1_post)
        z, c1, c2, c3, r2, act = _inproj_call(layer, xc, m_pre, w_in_p, conv_w, conv_b, brow, arow, bcol, acol)
        yf, yb = _ssd_call(layer, act, c1, c2, c3, r2)
        of, ob = _gdn_call(layer, act, c1, c2, c3, r2)
        xc = _outproj_call(layer, xc, yf, yb, act, z, of, ob, dexp, ssd_g, gdn_g, w_out16, m_post)
        xc = _ffn_call(layer, xc, f2_pre, f2_gu, f2_dn, f2_post)
        return xc

    out = lax.fori_loop(0, DEPTH, layer_fn, x.reshape(SEQ, D_MODEL).astype(F32))
    return out.reshape(x.shape)
```

```python
import functools

import jax
import jax.numpy as jnp
from jax import lax
from jax.experimental import pallas as pl
from jax.experimental.pallas import tpu as pltpu

F32 = jnp.float32
BF16 = jnp.bfloat16
HIGHEST = lax.Precision.HIGHEST

D_MODEL = 2048
SEQ = 8192
DEPTH = 4
EPS = 1e-6
D_FF = 5632
CONV_WIDTH = 5
SSD_HEADS = 32
SSD_HEAD_DIM = 64
SSD_WIDTH = 2048
SSD_GROUPS = 4
SSD_STATE = 128
SSD_CHUNK = 128
SSD_BC = 512
SSD_CONV_CH = SSD_WIDTH + 2 * SSD_BC
GDN_HEADS = 16
GDN_HEAD_DIM = 128
GDN_WIDTH = 2048
GDN_CONV_CH = 3 * GDN_WIDTH
D_IN_PROJ = 13440

LANES = 128
COL_ZS = 0
COL_ZG = 2048
COL_CONV = 4096
N_CONV = SSD_CONV_CH + GDN_CONV_CH
COL_GATE = COL_CONV + N_CONV
N_CONV_BLK = N_CONV // LANES
ACT_X = 0
ACT_B = ACT_X + SSD_WIDTH // LANES
ACT_C = ACT_B + SSD_BC // LANES
ACT_Q = ACT_C + SSD_BC // LANES
ACT_K = ACT_Q + GDN_HEADS
ACT_V = ACT_K + GDN_HEADS
assert ACT_V + GDN_HEADS == N_CONV_BLK
G_SSD = 0
G_GDN = 64
G_BETA = 96

GDN_SOLVE_CHUNK = 128
GDN_INV_SHIFT = 6
GDN_BLOCK = 512
assert 2 << GDN_INV_SHIFT == GDN_SOLVE_CHUNK
assert GDN_SOLVE_CHUNK == SSD_CHUNK
VMEM_LIMIT = 56 * 1024 * 1024


def _mm(a, b, precision=None):
    return jnp.dot(a, b, preferred_element_type=F32, precision=precision)


def _nt(a, b, precision=None):
    return lax.dot_general(a, b, (((1,), (1,)), ((), ())), preferred_element_type=F32, precision=precision)


def _tn(a, b):
    return lax.dot_general(a, b, (((0,), (0,)), ((), ())), preferred_element_type=F32)


def _rms(x, g):
    return x * lax.rsqrt(jnp.mean(x * x, axis=-1, keepdims=True) + EPS) * g


def _silu(x):
    return x * jax.nn.sigmoid(x)


def _softplus(x):
    return jnp.maximum(x, 0.0) + jnp.log1p(jnp.exp(-jnp.abs(x)))


def _params(sem):
    return pltpu.CompilerParams(dimension_semantics=sem, vmem_limit_bytes=VMEM_LIMIT)


def _ffn_kernel(l_ref, x_ref, pre_ref, wg_ref, wu_ref, wd_ref, post_ref, o_ref, h_ref, acc_ref, *, nj):
    j = pl.program_id(1)

    @pl.when(j == 0)
    def _():
        h_ref[...] = _rms(x_ref[...], pre_ref[...]).astype(BF16)
        acc_ref[...] = jnp.zeros_like(acc_ref)

    h = h_ref[...]
    gate = _mm(h, wg_ref[...])
    up = _mm(h, wu_ref[...])
    act = (_silu(gate) * up).astype(BF16)
    acc_ref[...] += _mm(act, wd_ref[...])

    @pl.when(j == nj - 1)
    def _():
        o_ref[...] = x_ref[...] + 0.5 * _rms(acc_ref[...], post_ref[...])


def _ffn_call(layer, x, pre_g, w_gu, w_down, post_g, *, tm=512, tf=512):
    seq, d = x.shape
    nj = D_FF // tf
    gs = pltpu.PrefetchScalarGridSpec(
        num_scalar_prefetch=1,
        grid=(seq // tm, nj),
        in_specs=[
            pl.BlockSpec((tm, d), lambda i, j, l: (i, 0)),
            pl.BlockSpec((None, 1, d), lambda i, j, l: (l[0], 0, 0)),
            pl.BlockSpec((None, d, tf), lambda i, j, l: (l[0], 0, j)),
            pl.BlockSpec((None, d, tf), lambda i, j, l: (l[0], 0, j + nj)),
            pl.BlockSpec((None, tf, d), lambda i, j, l: (l[0], j, 0)),
            pl.BlockSpec((None, 1, d), lambda i, j, l: (l[0], 0, 0)),
        ],
        out_specs=pl.BlockSpec((tm, d), lambda i, j, l: (i, 0)),
        scratch_shapes=[pltpu.VMEM((tm, d), BF16), pltpu.VMEM((tm, d), F32)],
    )
    return pl.pallas_call(
        functools.partial(_ffn_kernel, nj=nj),
        grid_spec=gs,
        out_shape=jax.ShapeDtypeStruct((seq, d), F32),
        compiler_params=_params(("parallel", "arbitrary")),
        name="ffn",
    )(layer, x, pre_g, w_gu, w_gu, w_down, post_g)


HALO = 16
CONV_SUB = 256


def _gate_block(raw, brow, arow, bcol, acol):
    q = SSD_CHUNK
    lane = lax.broadcasted_iota(jnp.int32, (q, LANES), 1)
    sp = _softplus(raw + brow)
    c1 = jnp.where(lane < G_BETA, sp, jax.nn.sigmoid(raw))
    a = jnp.where(lane < G_BETA, sp * (-jnp.exp(arow)), 0.0)

    ri = lax.broadcasted_iota(jnp.int32, (q, q), 0)
    ci = lax.broadcasted_iota(jnp.int32, (q, q), 1)
    lower = (ri >= ci).astype(F32)
    upper = (ri <= ci).astype(F32)
    ones = jnp.ones((q, q), F32)

    def is_fwd(idx):
        return (idx < G_SSD + SSD_HEADS) | ((idx >= G_GDN) & (idx < G_GDN + GDN_HEADS))

    c2 = jnp.where(is_fwd(lane), _mm(lower, a, HIGHEST), _mm(upper, a, HIGHEST))
    c3 = _mm(ones, a, HIGHEST)
    sub = lax.broadcasted_iota(jnp.int32, (LANES, q), 0)
    at = jnp.where(sub < G_BETA, _softplus(raw.T + bcol) * (-jnp.exp(acol)), 0.0)
    r2 = jnp.where(is_fwd(sub), _nt(at, lower, HIGHEST), _nt(at, upper, HIGHEST))
    return c1, c2, c3, r2


def _inproj_kernel(l_ref, xp_ref, x_ref, xn_ref, pre_ref, w_ref, wg_ref, cw_ref, cb_ref,
                   brow_ref, arow_ref, bcol_ref, acol_ref,
                   z_ref, c1_ref, c2_ref, c3_ref, r2_ref, act_ref, h_ref, *, tm, ni, nz):
    i = pl.program_id(0)
    j = pl.program_id(1)

    @pl.when(j == 0)
    def _():
        pre = pre_ref[...]
        h_ref[0:HALO, :] = jnp.where(i > 0, _rms(xp_ref[...], pre), 0.0).astype(BF16)
        h_ref[HALO:HALO + tm, :] = _rms(x_ref[...], pre).astype(BF16)
        h_ref[HALO + tm:, :] = jnp.where(i < ni - 1, _rms(xn_ref[...], pre), 0.0).astype(BF16)
        gate = _mm(h_ref[HALO:HALO + tm, :], wg_ref[...])
        for c in range(tm // SSD_CHUNK):
            rows = slice(c * SSD_CHUNK, (c + 1) * SSD_CHUNK)
            c1, c2, c3, r2 = _gate_block(gate[rows], brow_ref[...], arow_ref[...], bcol_ref[...], acol_ref[...])
            c1_ref[rows, :] = c1
            c2_ref[rows, :] = c2
            c3_ref[rows, :] = c3
            r2_ref[:, rows] = r2

    @pl.when(j < nz)
    def _():
        z_ref[...] = _mm(h_ref[HALO:HALO + tm, :], w_ref[...]).astype(BF16)

    @pl.when(j >= nz)
    def _():
        pad = CONV_WIDTH // 2
        h = h_ref[...]
        for s in range(w_ref.shape[1] // CONV_SUB):
            cols = slice(s * CONV_SUB, (s + 1) * CONV_SUB)
            acc = _mm(h, w_ref[:, cols])
            y = cb_ref[:, cols]
            for t in range(CONV_WIDTH):
                tap = acc if t == pad else pltpu.roll(acc, (pad - t) % acc.shape[0], 0)
                y = y + cw_ref[t:t + 1, cols] * tap[HALO:HALO + tm, :]
            y = _silu(y).astype(BF16)
            for b in range(CONV_SUB // LANES):
                act_ref[s * (CONV_SUB // LANES) + b] = y[:, b * LANES:(b + 1) * LANES]


def _inproj_call(layer, x, pre_g, w_in, conv_w, conv_b, brow, arow, bcol, acol, *, tm=1024, tn=1024):
    seq, d = x.shape
    ni = seq // tm
    nz = COL_CONV // tn
    nj = COL_GATE // tn
    hb = tm // HALO
    row_p = pl.BlockSpec((None, 1, LANES), lambda i, j, l: (l[0], 0, 0))
    col_p = pl.BlockSpec((None, LANES, 1), lambda i, j, l: (l[0], 0, 0))
    colform = pl.BlockSpec((tm, LANES), lambda i, j, l: (i, 0))
    col_shape = jax.ShapeDtypeStruct((seq, LANES), F32)
    gs = pltpu.PrefetchScalarGridSpec(
        num_scalar_prefetch=1,
        grid=(ni, nj),
        in_specs=[
            pl.BlockSpec((HALO, d), lambda i, j, l: (jnp.maximum(i * hb - 1, 0), 0)),
            pl.BlockSpec((tm, d), lambda i, j, l: (i, 0)),
            pl.BlockSpec((HALO, d), lambda i, j, l: (jnp.minimum((i + 1) * hb, seq // HALO - 1), 0)),
            pl.BlockSpec((None, 1, d), lambda i, j, l: (l[0], 0, 0)),
            pl.BlockSpec((None, d, tn), lambda i, j, l: (l[0], 0, j)),
            pl.BlockSpec((None, d, LANES), lambda i, j, l: (l[0], 0, COL_GATE // LANES)),
            pl.BlockSpec((None, CONV_WIDTH, tn), lambda i, j, l: (l[0], 0, jnp.maximum(j - nz, 0))),
            pl.BlockSpec((None, 1, tn), lambda i, j, l: (l[0], 0, jnp.maximum(j - nz, 0))),
            row_p, row_p, col_p, col_p,
        ],
        out_specs=[
            pl.BlockSpec((tm, tn), lambda i, j, l: (i, jnp.minimum(j, nz - 1))),
            colform, colform, colform,
            pl.BlockSpec((LANES, tm), lambda i, j, l: (0, i)),
            pl.BlockSpec((tn // LANES, tm, LANES), lambda i, j, l: (jnp.maximum(j - nz, 0), i, 0)),
        ],
        scratch_shapes=[pltpu.VMEM((tm + 2 * HALO, d), BF16)],
    )
    return pl.pallas_call(
        functools.partial(_inproj_kernel, tm=tm, ni=ni, nz=nz),
        grid_spec=gs,
        out_shape=[jax.ShapeDtypeStruct((seq, COL_CONV), BF16),
                   col_shape, col_shape, col_shape, jax.ShapeDtypeStruct((LANES, seq), F32),
                   jax.ShapeDtypeStruct((N_CONV_BLK, seq, LANES), BF16)],
        compiler_params=_params(("parallel", "arbitrary")),
        name="inproj",
    )(layer, x, x, x, pre_g, w_in, w_in, conv_w, conv_b, brow, arow, bcol, acol)


def _ssd_dir(x_ref, b_ref, c_ref, c1_ref, c2_ref, c3_ref, r2_ref, s_ref, y_ref, *, d):
    q = SSD_CHUNK
    gw = SSD_WIDTH // SSD_GROUPS
    hpg = SSD_HEADS // SSD_GROUPS
    head_shift = SSD_HEAD_DIM.bit_length() - 1
    dt_all = c1_ref[...]
    cs_all = c2_ref[...]
    tot_all = c3_ref[...]
    etot = jnp.exp(tot_all[0:16])
    etot_hi = etot.astype(BF16)
    etot_lo = (etot - etot_hi.astype(F32)).astype(BF16)
    stacked = jnp.concatenate([dt_all.astype(BF16), jnp.exp(cs_all).astype(BF16),
                               jnp.exp(tot_all - cs_all).astype(BF16), etot_hi, etot_lo], axis=0)
    ri = lax.broadcasted_iota(jnp.int32, (q, q), 0)
    ci = lax.broadcasted_iota(jnp.int32, (q, q), 1)
    mask = (ri >= ci) if d == 0 else (ri <= ci)
    kk = lax.broadcasted_iota(jnp.int32, (LANES, gw), 0)
    nn = lax.broadcasted_iota(jnp.int32, (LANES, gw), 1)
    lane = lax.broadcasted_iota(jnp.int32, (q, LANES), 1)
    for g in range(SSD_GROUPS):
        base = G_SSD + SSD_HEADS * d + hpg * g
        sel = (kk == base + (nn >> head_shift)).astype(BF16)
        ex = _mm(stacked, sel)
        dt_e, ecs_e, dte_e = ex[0:q], ex[q:2 * q], ex[2 * q:3 * q]
        etot_e = ex[3 * q:3 * q + 1] + ex[3 * q + 16:3 * q + 17]
        xg = jnp.concatenate([x_ref[gw // LANES * g + p] for p in range(gw // LANES)], axis=1).astype(F32)
        xdt = xg * dt_e
        xdt16 = xdt.astype(BF16)
        xdte16 = (xdt * dte_e).astype(BF16)
        bg = b_ref[g]
        cg = c_ref[g]
        cb = _nt(cg, bg)
        s_prev = s_ref[d, g]
        y_off = _mm(cg, s_prev.astype(BF16)) * ecs_e
        s_ref[d, g] = s_prev * etot_e + _tn(bg, xdte16)
        ys = []
        for pair in range(hpg // 2):
            ms = []
            for hh in (2 * pair, 2 * pair + 1):
                r = base + hh
                dec = jnp.exp(jnp.where(mask, cs_all[:, r:r + 1] - r2_ref[r:r + 1, :], -jnp.inf))
                ms.append((cb * dec).astype(BF16))
            lhs = jnp.concatenate(ms, axis=1)
            xp = xdt16[:, pair * LANES:(pair + 1) * LANES]
            zero = jnp.zeros_like(xp)
            rhs = jnp.concatenate([jnp.where(lane < SSD_HEAD_DIM, xp, zero),
                                   jnp.where(lane >= SSD_HEAD_DIM, xp, zero)], axis=0)
            ys.append(_mm(lhs, rhs))
        y = jnp.concatenate(ys, axis=1) + y_off
        y_ref[:, g * gw:(g + 1) * gw] = y.astype(BF16)


def _ssd_kernel(l_ref,
                xf_ref, bf_ref, cf_ref, c1f_ref, c2f_ref, c3f_ref, r2f_ref,
                xb_ref, bb_ref, cb_ref, c1b_ref, c2b_ref, c3b_ref, r2b_ref,
                yf_ref, yb_ref, s_ref):
    @pl.when(pl.program_id(0) == 0)
    def _():
        s_ref[...] = jnp.zeros_like(s_ref)

    _ssd_dir(xf_ref, bf_ref, cf_ref, c1f_ref, c2f_ref, c3f_ref, r2f_ref, s_ref, yf_ref, d=0)
    _ssd_dir(xb_ref, bb_ref, cb_ref, c1b_ref, c2b_ref, c3b_ref, r2b_ref, s_ref, yb_ref, d=1)


def _ssd_call(layer, act, c1, c2, c3, r2):
    seq = c1.shape[0]
    q = SSD_CHUNK
    nc = seq // q
    nx, nbc = SSD_WIDTH // LANES, SSD_BC // LANES

    def dir_specs(rowblk):
        return [
            pl.BlockSpec((nx, q, LANES), lambda i, l: (ACT_X // nx, rowblk(i), 0)),
            pl.BlockSpec((nbc, q, LANES), lambda i, l: (ACT_B // nbc, rowblk(i), 0)),
            pl.BlockSpec((nbc, q, LANES), lambda i, l: (ACT_C // nbc, rowblk(i), 0)),
            pl.BlockSpec((q, LANES), lambda i, l: (rowblk(i), 0)),
            pl.BlockSpec((q, LANES), lambda i, l: (rowblk(i), 0)),
            pl.BlockSpec((q, LANES), lambda i, l: (rowblk(i), 0)),
            pl.BlockSpec((LANES, q), lambda i, l: (0, rowblk(i))),
        ]

    fwd = lambda i: i
    bwd = lambda i: nc - 1 - i
    gs = pltpu.PrefetchScalarGridSpec(
        num_scalar_prefetch=1,
        grid=(nc,),
        in_specs=dir_specs(fwd) + dir_specs(bwd),
        out_specs=[pl.BlockSpec((q, SSD_WIDTH), lambda i, l: (fwd(i), 0)),
                   pl.BlockSpec((q, SSD_WIDTH), lambda i, l: (bwd(i), 0))],
        scratch_shapes=[pltpu.VMEM((2, SSD_GROUPS, SSD_STATE, SSD_WIDTH // SSD_GROUPS), F32)],
    )
    y_shape = jax.ShapeDtypeStruct((seq, SSD_WIDTH), BF16)
    args = (act, act, act, c1, c2, c3, r2)
    return pl.pallas_call(
        _ssd_kernel,
        grid_spec=gs,
        out_shape=[y_shape, y_shape],
        compiler_params=_params(("arbitrary",)),
        name="ssd",
    )(layer, *args, *args)


def _gdn_prep(q_ref, k_ref, v_ref, c1_ref, c2_ref, c3_ref, r2_ref, *, j, d, head, rows):
    c = GDN_SOLVE_CHUNK
    dh = GDN_HEAD_DIM
    q = q_ref[j, rows, :].astype(F32)
    k = k_ref[j, rows, :].astype(F32)
    v = v_ref[j, rows, :].astype(F32)
    q = q * (lax.rsqrt(jnp.sum(q * q, axis=-1, keepdims=True) + EPS) * (dh ** -0.5))
    k = k * lax.rsqrt(jnp.sum(k * k, axis=-1, keepdims=True) + EPS)

    lane = lax.broadcasted_iota(jnp.int32, (c, LANES), 1)
    ig = G_GDN + GDN_HEADS * d + head
    ib = G_BETA + GDN_HEADS * d + head
    gcs = jnp.sum(jnp.where(lane == ig, c2_ref[rows, :], 0.0), axis=-1, keepdims=True)
    gtot = jnp.sum(jnp.where(lane == ig, c3_ref[rows, :], 0.0), axis=-1, keepdims=True)
    beta = jnp.sum(jnp.where(lane == ib, c1_ref[rows, :], 0.0), axis=-1, keepdims=True)
    grow = r2_ref[pl.ds(ig, 1), :][:, rows]

    ri = lax.broadcasted_iota(jnp.int32, (c, c), 0)
    ci = lax.broadcasted_iota(jnp.int32, (c, c), 1)
    incl = (ri >= ci) if d == 0 else (ri <= ci)
    strict = (ri > ci) if d == 0 else (ri < ci)
    decay = jnp.exp(jnp.where(incl, gcs - grow, -jnp.inf))

    kb = k * beta
    eg = jnp.exp(gcs)
    kq = _nt(jnp.concatenate([kb, q], axis=0).astype(BF16), k.astype(BF16))
    p = jnp.where(strict, -(kq[:c] * decay), 0.0)
    half = c // 2
    left = lax.broadcasted_iota(jnp.int32, (half, LANES), 1) < half
    return dict(
        q=jnp.where(left, p[:half], p[half:]),
        poff16=jnp.where(left, p[half:], p[:half]).astype(BF16),
        attn16=(kq[c:] * decay).astype(BF16),
        rhs16=jnp.concatenate([v * beta, kb * eg], axis=1).astype(BF16),
        qeg=q * eg,
        ket16=(k * jnp.exp(gtot - gcs)).T.astype(BF16),
        ld=jnp.exp(gtot[0:1, :]),
    )


def _gdn_kernel(l_ref,
                qf_ref, kf_ref, vf_ref, c1f_ref, c2f_ref, c3f_ref, r2f_ref,
                qb_ref, kb_ref, vb_ref, c1b_ref, c2b_ref, c3b_ref, r2b_ref,
                of_ref, ob_ref, s_ref, *, hb):
    @pl.when(pl.program_id(1) == 0)
    def _():
        s_ref[...] = jnp.zeros_like(s_ref)

    c = GDN_SOLVE_CHUNK
    dh = GDN_HEAD_DIM
    ncb = GDN_BLOCK // c
    in_refs = ((qf_ref, kf_ref, vf_ref, c1f_ref, c2f_ref, c3f_ref, r2f_ref),
               (qb_ref, kb_ref, vb_ref, c1b_ref, c2b_ref, c3b_ref, r2b_ref))
    out_refs = (of_ref, ob_ref)
    streams = [(j, d) for j in range(hb) for d in range(2)]
    state = {sd: s_ref[sd[1], sd[0]] for sd in streams}

    half = c // 2
    lane_h = lax.broadcasted_iota(jnp.int32, (half, LANES), 1)
    row_h = lax.broadcasted_iota(jnp.int32, (half, LANES), 0)
    left = lane_h < half
    eye2 = (lane_h == row_h) | (lane_h == row_h + half)

    def blockdiag16(x):
        return jnp.concatenate([jnp.where(left, x, 0.0), jnp.where(left, 0.0, x)], axis=0).astype(BF16)

    def scan_read(units):
        for (j, d), u in units.items():
            ws = _mm(u["lhs16"], state[(j, d)].astype(BF16))
            u["v_new16"] = (u["u"] - ws[:c]).astype(BF16)
            u["qs"] = ws[c:]

    def scan_write(units):
        for (j, d), u in units.items():
            state[(j, d)] = state[(j, d)] * u["ld"] + _mm(u["ket16"], u["v_new16"])
            out_refs[d][j, u["rows"], :] = (u["qs"] + _mm(u["attn16"], u["v_new16"])).astype(BF16)

    prev = None
    for phase in range(ncb):
        units = {}
        for (j, d) in streams:
            cc = phase if d == 0 else ncb - 1 - phase
            rows = slice(cc * c, (cc + 1) * c)
            units[(j, d)] = _gdn_prep(*in_refs[d], j=j, d=d, head=pl.program_id(0) * hb + j, rows=rows)
            units[(j, d)]["rows"] = rows
        woven = [] if prev is None else [functools.partial(scan_read, prev), functools.partial(scan_write, prev)]
        for u in units.values():
            u["t"] = jnp.where(eye2, 1.0, u["q"])
            u["q"] = _mm(u["q"].astype(BF16), blockdiag16(u["q"]))
        if woven:
            woven[0]()
        for level in range(1, GDN_INV_SHIFT):
            for u in units.values():
                rhs = blockdiag16(u["q"])
                if level == GDN_INV_SHIFT - 1:
                    u["t"] = u["t"] + _mm(u["t"].astype(BF16), rhs)
                else:
                    qt = _mm(jnp.concatenate([u["q"], u["t"]], axis=0).astype(BF16), rhs)
                    u["q"] = qt[:half]
                    u["t"] = u["t"] + qt[half:]
            if level < len(woven):
                woven[level]()
        for u in units.values():
            u["tbd16"] = blockdiag16(u["t"])
            u["y16"] = _mm(u["poff16"], u["tbd16"]).astype(BF16)
        zeros = jnp.zeros((half, LANES), BF16)
        for (j, d), u in units.items():
            tbd16 = u["tbd16"]
            if d == 0:
                z = _mm(tbd16[half:], jnp.concatenate([zeros, u["y16"]], axis=0))
                u["t16"] = jnp.concatenate([tbd16[:half], jnp.where(left, z, u["t"]).astype(BF16)], axis=0)
            else:
                z = _mm(tbd16[:half], jnp.concatenate([u["y16"], zeros], axis=0))
                u["t16"] = jnp.concatenate([jnp.where(left, u["t"], z).astype(BF16), tbd16[half:]], axis=0)
        for u in units.values():
            sol = _mm(u["t16"], u["rhs16"])
            u["u"] = sol[:, :dh]
            u["lhs16"] = jnp.concatenate([sol[:, dh:], u["qeg"]], axis=0).astype(BF16)
        prev = units
    scan_read(prev)
    scan_write(prev)
    for (j, d) in streams:
        s_ref[d, j] = state[(j, d)]


def _gdn_call(layer, act, c1, c2, c3, r2, *, hb=8):
    seq = c1.shape[0]
    r = GDN_BLOCK
    nb = seq // r
    q0, k0, v0 = ACT_Q // hb, ACT_K // hb, ACT_V // hb

    def dir_specs(rowblk):
        return [
            pl.BlockSpec((hb, r, LANES), lambda h, i, l: (q0 + h, rowblk(i), 0)),
            pl.BlockSpec((hb, r, LANES), lambda h, i, l: (k0 + h, rowblk(i), 0)),
            pl.BlockSpec((hb, r, LANES), lambda h, i, l: (v0 + h, rowblk(i), 0)),
            pl.BlockSpec((r, LANES), lambda h, i, l: (rowblk(i), 0)),
            pl.BlockSpec((r, LANES), lambda h, i, l: (rowblk(i), 0)),
            pl.BlockSpec((r, LANES), lambda h, i, l: (rowblk(i), 0)),
            pl.BlockSpec((LANES, r), lambda h, i, l: (0, rowblk(i))),
        ]

    fwd = lambda i: i
    bwd = lambda i: nb - 1 - i
    gs = pltpu.PrefetchScalarGridSpec(
        num_scalar_prefetch=1,
        grid=(GDN_HEADS // hb, nb),
        in_specs=dir_specs(fwd) + dir_specs(bwd),
        out_specs=[pl.BlockSpec((hb, r, LANES), lambda h, i, l: (h, fwd(i), 0)),
                   pl.BlockSpec((hb, r, LANES), lambda h, i, l: (h, bwd(i), 0))],
        scratch_shapes=[pltpu.VMEM((2, hb, GDN_HEAD_DIM, GDN_HEAD_DIM), F32)],
    )
    o_shape = jax.ShapeDtypeStruct((GDN_HEADS, seq, LANES), BF16)
    args = (act, act, act, c1, c2, c3, r2)
    return pl.pallas_call(
        functools.partial(_gdn_kernel, hb=hb),
        grid_spec=gs,
        out_shape=[o_shape, o_shape],
        compiler_params=_params(("parallel", "arbitrary")),
        name="gdn",
    )(layer, *args, *args)


def _outproj_kernel(l_ref, x_ref, yf_ref, yb_ref, xa_ref, zs_ref, zg_ref, of_ref, ob_ref,
                    dexp_ref, sg_ref, gg_ref, w_ref, post_ref, o_ref):
    gw = SSD_WIDTH // SSD_GROUPS
    hq = gw // LANES
    acc = None
    for g in range(SSD_GROUPS):
        sl = slice(g * gw, (g + 1) * gw)
        xg = jnp.concatenate([xa_ref[hq * g + p] for p in range(hq)], axis=1).astype(F32)
        y = yf_ref[:, sl].astype(F32) + yb_ref[:, sl].astype(F32) + dexp_ref[:, sl] * xg
        y = y * _silu(zs_ref[:, sl].astype(F32))
        part = _mm(_rms(y, sg_ref[:, sl]).astype(BF16), w_ref[sl, :])
        acc = part if acc is None else acc + part
    for g in range(GDN_WIDTH // gw):
        os = []
        for h in range(hq * g, hq * (g + 1)):
            o = of_ref[h].astype(F32) + ob_ref[h].astype(F32)
            o = _rms(o, gg_ref[...]) * _silu(zg_ref[:, h * LANES:(h + 1) * LANES].astype(F32))
            os.append(o.astype(BF16))
        acc = acc + _mm(jnp.concatenate(os, axis=1), w_ref[SSD_WIDTH + g * gw:SSD_WIDTH + (g + 1) * gw, :])
    o_ref[...] = x_ref[...] + _rms(acc, post_ref[...])


def _outproj_call(layer, x, yf, yb, act, z, of, ob, dexp, ssd_g, gdn_g, w_out, post_g, *, tm=256):
    seq, d = x.shape
    row_d = pl.BlockSpec((None, 1, d), lambda i, l: (l[0], 0, 0))
    gs = pltpu.PrefetchScalarGridSpec(
        num_scalar_prefetch=1,
        grid=(seq // tm,),
        in_specs=[
            pl.BlockSpec((tm, d), lambda i, l: (i, 0)),
            pl.BlockSpec((tm, SSD_WIDTH), lambda i, l: (i, 0)),
            pl.BlockSpec((tm, SSD_WIDTH), lambda i, l: (i, 0)),
            pl.BlockSpec((SSD_WIDTH // LANES, tm, LANES), lambda i, l: (ACT_X, i, 0)),
            pl.BlockSpec((tm, SSD_WIDTH), lambda i, l: (i, COL_ZS // SSD_WIDTH)),
            pl.BlockSpec((tm, GDN_WIDTH), lambda i, l: (i, COL_ZG // GDN_WIDTH)),
            pl.BlockSpec((GDN_HEADS, tm, LANES), lambda i, l: (0, i, 0)),
            pl.BlockSpec((GDN_HEADS, tm, LANES), lambda i, l: (0, i, 0)),
            row_d, row_d,
            pl.BlockSpec((None, 1, LANES), lambda i, l: (l[0], 0, 0)),
            pl.BlockSpec((None, 2 * d, d), lambda i, l: (l[0], 0, 0), pipeline_mode=pl.Buffered(1)),
            row_d,
        ],
        out_specs=pl.BlockSpec((tm, d), lambda i, l: (i, 0)),
    )
    return pl.pallas_call(
        _outproj_kernel,
        grid_spec=gs,
        out_shape=jax.ShapeDtypeStruct((seq, d), F32),
        compiler_params=_params(("parallel",)),
        name="outproj",
    )(layer, x, yf, yb, act, z, z, of, ob, dexp, ssd_g, gdn_g, w_out, post_g)


def kernel(x, ffn1_pre_g, ffn1_w_gu, ffn1_w_down, ffn1_post_g, mix_pre_g, w_in, ssd_conv_w, ssd_conv_b,
           ssd_dt_bias, ssd_a_log, ssd_d, ssd_norm_g, gdn_conv_w, gdn_dt_bias, gdn_a_log, gdn_norm_g, w_out,
           mix_post_g, ffn2_pre_g, ffn2_w_gu, ffn2_w_down, ffn2_post_g):
    row = lambda g: g.astype(F32).reshape(DEPTH, 1, -1)

    s0, s1, s2 = SSD_WIDTH, SSD_WIDTH + SSD_CONV_CH, SSD_WIDTH + SSD_CONV_CH + 2 * SSD_HEADS
    g1, g2 = s2 + GDN_CONV_CH, s2 + GDN_CONV_CH + GDN_WIDTH
    w_in16 = lax.optimization_barrier(w_in.astype(BF16))
    w_in_p = jnp.concatenate([w_in16[..., a:b]
                              for a, b in ((0, s0), (g1, g2), (s0, s1), (s2, g1), (s1, s2), (g2, D_IN_PROJ))], axis=-1)
    conv_w = jnp.concatenate([ssd_conv_w, gdn_conv_w], axis=-1).astype(F32)
    conv_b = jnp.concatenate([ssd_conv_b.astype(F32), jnp.zeros((DEPTH, GDN_CONV_CH), F32)], axis=-1)[:, None, :]
    pad = jnp.zeros((DEPTH, LANES - G_BETA), F32)
    bias = jnp.concatenate([ssd_dt_bias.reshape(DEPTH, -1), gdn_dt_bias.reshape(DEPTH, -1), pad], axis=-1)
    alog = jnp.concatenate([ssd_a_log.reshape(DEPTH, -1), gdn_a_log.reshape(DEPTH, -1), pad], axis=-1)
    brow, arow = bias[:, None, :], alog[:, None, :]
    bcol, acol = bias[:, :, None], alog[:, :, None]
    dexp = jnp.repeat(ssd_d.astype(F32), SSD_HEAD_DIM, axis=-1)[:, None, :]
    f1_gu, f1_dn = ffn1_w_gu.astype(BF16), ffn1_w_down.astype(BF16)
    f2_gu, f2_dn = ffn2_w_gu.astype(BF16), ffn2_w_down.astype(BF16)
    w_out16 = w_out.astype(BF16)
    f1_pre, f1_post, f2_pre, f2_post = row(ffn1_pre_g), row(ffn1_post_g), row(ffn2_pre_g), row(ffn2_post_g)
    m_pre, m_post, ssd_g, gdn_g = row(mix_pre_g), row(mix_post_g), row(ssd_norm_g), row(gdn_norm_g)

    def layer_fn(i, xc):
        layer = jnp.reshape(i, (1,)).astype(jnp.int32)
        xc = _ffn_call(layer, xc, f1_pre, f1_gu, f1_dn, f1_post)
        z, c1, c2, c3, r2, act = _inproj_call(layer, xc, m_pre, w_in_p, conv_w, conv_b, brow, arow, bcol, acol)
        yf, yb = _ssd_call(layer, act, c1, c2, c3, r2)
        of, ob = _gdn_call(layer, act, c1, c2, c3, r2)
        xc = _outproj_call(layer, xc, yf, yb, act, z, of, ob, dexp, ssd_g, gdn_g, w_out16, m_post)
        xc = _ffn_call(layer, xc, f2_pre, f2_gu, f2_dn, f2_post)
        return xc

    out = lax.fori_loop(0, DEPTH, layer_fn, x.reshape(SEQ, D_MODEL).astype(F32))
    return out.reshape(x.shape)
```

```python
import functools

import jax
import jax.numpy as jnp
from jax import lax
from jax.experimental import pallas as pl
from jax.experimental.pallas import tpu as pltpu

F32 = jnp.float32
BF16 = jnp.bfloat16
HIGHEST = lax.Precision.HIGHEST

D_MODEL = 2048
SEQ = 8192
DEPTH = 4
EPS = 1e-6
D_FF = 5632
CONV_WIDTH = 5
SSD_HEADS = 32
SSD_HEAD_DIM = 64
SSD_WIDTH = 2048
SSD_GROUPS = 4
SSD_STATE = 128
SSD_CHUNK = 128
SSD_BC = 512
SSD_CONV_CH = SSD_WIDTH + 2 * SSD_BC
GDN_HEADS = 16
GDN_HEAD_DIM = 128
GDN_WIDTH = 2048
GDN_CONV_CH = 3 * GDN_WIDTH
D_IN_PROJ = 13440

LANES = 128
COL_ZS = 0
COL_ZG = 2048
COL_CONV = 4096
N_CONV = SSD_CONV_CH + GDN_CONV_CH
COL_GATE = COL_CONV + N_CONV
N_CONV_BLK = N_CONV // LANES
ACT_X = 0
ACT_B = ACT_X + SSD_WIDTH // LANES
ACT_C = ACT_B + SSD_BC // LANES
ACT_Q = ACT_C + SSD_BC // LANES
ACT_K = ACT_Q + GDN_HEADS
ACT_V = ACT_K + GDN_HEADS
assert ACT_V + GDN_HEADS == N_CONV_BLK
G_SSD = 0
G_GDN = 64
G_BETA = 96

GDN_SOLVE_CHUNK = 128
GDN_INV_SHIFT = 6
GDN_BLOCK = 512
assert 2 << GDN_INV_SHIFT == GDN_SOLVE_CHUNK
assert GDN_SOLVE_CHUNK == SSD_CHUNK
VMEM_LIMIT = 56 * 1024 * 1024


def _mm(a, b, precision=None):
    return jnp.dot(a, b, preferred_element_type=F32, precision=precision)


def _nt(a, b, precision=None):
    return lax.dot_general(a, b, (((1,), (1,)), ((), ())), preferred_element_type=F32, precision=precision)


def _tn(a, b):
    return lax.dot_general(a, b, (((0,), (0,)), ((), ())), preferred_element_type=F32)


def _rms(x, g):
    return x * lax.rsqrt(jnp.mean(x * x, axis=-1, keepdims=True) + EPS) * g


def _silu(x):
    return x * jax.nn.sigmoid(x)


def _softplus(x):
    return jnp.maximum(x, 0.0) + jnp.log1p(jnp.exp(-jnp.abs(x)))


def _params(sem):
    return pltpu.CompilerParams(dimension_semantics=sem, vmem_limit_bytes=VMEM_LIMIT)


def _ffn_kernel(l_ref, x_ref, pre_ref, wg_ref, wu_ref, wd_ref, post_ref, o_ref, h_ref, acc_ref, *, nj):
    j = pl.program_id(1)

    @pl.when(j == 0)
    def _():
        h_ref[...] = _rms(x_ref[...], pre_ref[...]).astype(BF16)
        acc_ref[...] = jnp.zeros_like(acc_ref)

    h = h_ref[...]
    gate = _mm(h, wg_ref[...])
    up = _mm(h, wu_ref[...])
    act = (_silu(gate) * up).astype(BF16)
    acc_ref[...] += _mm(act, wd_ref[...])

    @pl.when(j == nj - 1)
    def _():
        o_ref[...] = x_ref[...] + 0.5 * _rms(acc_ref[...], post_ref[...])


def _ffn_call(layer, x, pre_g, w_gu, w_down, post_g, *, tm=512, tf=512):
    seq, d = x.shape
    nj = D_FF // tf
    gs = pltpu.PrefetchScalarGridSpec(
        num_scalar_prefetch=1,
        grid=(seq // tm, nj),
        in_specs=[
            pl.BlockSpec((tm, d), lambda i, j, l: (i, 0)),
            pl.BlockSpec((None, 1, d), lambda i, j, l: (l[0], 0, 0)),
            pl.BlockSpec((None, d, tf), lambda i, j, l: (l[0], 0, j)),
            pl.BlockSpec((None, d, tf), lambda i, j, l: (l[0], 0, j + nj)),
            pl.BlockSpec((None, tf, d), lambda i, j, l: (l[0], j, 0)),
            pl.BlockSpec((None, 1, d), lambda i, j, l: (l[0], 0, 0)),
        ],
        out_specs=pl.BlockSpec((tm, d), lambda i, j, l: (i, 0)),
        scratch_shapes=[pltpu.VMEM((tm, d), BF16), pltpu.VMEM((tm, d), F32)],
    )
    return pl.pallas_call(
        functools.partial(_ffn_kernel, nj=nj),
        grid_spec=gs,
        out_shape=jax.ShapeDtypeStruct((seq, d), F32),
        compiler_params=_params(("parallel", "arbitrary")),
        name="ffn",
    )(layer, x, pre_g, w_gu, w_gu, w_down, post_g)


HALO = 16
CONV_SUB = 256


def _gate_block(raw, brow, arow, bcol, acol):
    q = SSD_CHUNK
    lane = lax.broadcasted_iota(jnp.int32, (q, LANES), 1)
    sp = _softplus(raw + brow)
    c1 = jnp.where(lane < G_BETA, sp, jax.nn.sigmoid(raw))
    a = jnp.where(lane < G_BETA, sp * (-jnp.exp(arow)), 0.0)

    ri = lax.broadcasted_iota(jnp.int32, (q, q), 0)
    ci = lax.broadcasted_iota(jnp.int32, (q, q), 1)
    lower = (ri >= ci).astype(F32)
    upper = (ri <= ci).astype(F32)
    ones = jnp.ones((q, q), F32)

    def is_fwd(idx):
        return (idx < G_SSD + SSD_HEADS) | ((idx >= G_GDN) & (idx < G_GDN + GDN_HEADS))

    c2 = jnp.where(is_fwd(lane), _mm(lower, a, HIGHEST), _mm(upper, a, HIGHEST))
    c3 = _mm(ones, a, HIGHEST)
    sub = lax.broadcasted_iota(jnp.int32, (LANES, q), 0)
    at = jnp.where(sub < G_BETA, _softplus(raw.T + bcol) * (-jnp.exp(acol)), 0.0)
    r2 = jnp.where(is_fwd(sub), _nt(at, lower, HIGHEST), _nt(at, upper, HIGHEST))
    return c1, c2, c3, r2


def _inproj_kernel(l_ref, xp_ref, x_ref, xn_ref, pre_ref, wa_ref, wb_ref, wg_ref, cw_ref, cb_ref,
                   brow_ref, arow_ref, bcol_ref, acol_ref,
                   z_ref, c1_ref, c2_ref, c3_ref, r2_ref, act_ref, h_ref, *, tm, ni, nz):
    i = pl.program_id(0)
    j = pl.program_id(1)

    @pl.when(j == 0)
    def _():
        pre = pre_ref[...]
        h_ref[0:HALO, :] = jnp.where(i > 0, _rms(xp_ref[...], pre), 0.0).astype(BF16)
        h_ref[HALO:HALO + tm, :] = _rms(x_ref[...], pre).astype(BF16)
        h_ref[HALO + tm:, :] = jnp.where(i < ni - 1, _rms(xn_ref[...], pre), 0.0).astype(BF16)
        gate = _mm(h_ref[HALO:HALO + tm, :], wg_ref[...])
        for c in range(tm // SSD_CHUNK):
            rows = slice(c * SSD_CHUNK, (c + 1) * SSD_CHUNK)
            c1, c2, c3, r2 = _gate_block(gate[rows], brow_ref[...], arow_ref[...], bcol_ref[...], acol_ref[...])
            c1_ref[rows, :] = c1
            c2_ref[rows, :] = c2
            c3_ref[rows, :] = c3
            r2_ref[:, rows] = r2

    from_a = (j < nz // 2) | ((j >= nz) & (j < nz + SSD_CONV_CH // wa_ref.shape[1]))

    def z_step(w_ref):
        z_ref[...] = _mm(h_ref[HALO:HALO + tm, :], w_ref[...]).astype(BF16)

    def conv_step(w_ref):
        pad = CONV_WIDTH // 2
        h = h_ref[...]
        for s in range(w_ref.shape[1] // CONV_SUB):
            cols = slice(s * CONV_SUB, (s + 1) * CONV_SUB)
            acc = _mm(h, w_ref[:, cols])
            y = cb_ref[:, cols]
            for t in range(CONV_WIDTH):
                tap = acc if t == pad else pltpu.roll(acc, (pad - t) % acc.shape[0], 0)
                y = y + cw_ref[t:t + 1, cols] * tap[HALO:HALO + tm, :]
            y = _silu(y).astype(BF16)
            for b in range(CONV_SUB // LANES):
                act_ref[s * (CONV_SUB // LANES) + b] = y[:, b * LANES:(b + 1) * LANES]

    for step, is_z in ((z_step, True), (conv_step, False)):
        for w_ref, is_a in ((wa_ref, True), (wb_ref, False)):
            pl.when(((j < nz) == is_z) & (from_a == is_a))(functools.partial(step, w_ref))


def _inproj_call(layer, x, pre_g, w_ssd, w_gdn, w_gate, conv_w, conv_b, brow, arow, bcol, acol, *, tm=1024, tn=1024):
    seq, d = x.shape
    ni = seq // tm
    nz = COL_CONV // tn
    nj = COL_GATE // tn
    hb = tm // HALO
    nzh, nxa, nq = nz // 2, SSD_CONV_CH // tn, GDN_CONV_CH // tn

    def blk_ssd(j):
        return jnp.where(j < nzh, j, jnp.clip(j - nzh, nzh, nzh + nxa - 1))

    def blk_gdn(j):
        return jnp.where(j < nz, nq + jnp.clip(j - nzh, 0, nzh - 1), jnp.clip(j - nz - nxa, 0, nq - 1))

    row_p = pl.BlockSpec((None, 1, LANES), lambda i, j, l: (l[0], 0, 0))
    col_p = pl.BlockSpec((None, LANES, 1), lambda i, j, l: (l[0], 0, 0))
    colform = pl.BlockSpec((tm, LANES), lambda i, j, l: (i, 0))
    col_shape = jax.ShapeDtypeStruct((seq, LANES), F32)
    gs = pltpu.PrefetchScalarGridSpec(
        num_scalar_prefetch=1,
        grid=(ni, nj),
        in_specs=[
            pl.BlockSpec((HALO, d), lambda i, j, l: (jnp.maximum(i * hb - 1, 0), 0)),
            pl.BlockSpec((tm, d), lambda i, j, l: (i, 0)),
            pl.BlockSpec((HALO, d), lambda i, j, l: (jnp.minimum((i + 1) * hb, seq // HALO - 1), 0)),
            pl.BlockSpec((None, 1, d), lambda i, j, l: (l[0], 0, 0)),
            pl.BlockSpec((None, d, tn), lambda i, j, l: (l[0], 0, blk_ssd(j))),
            pl.BlockSpec((None, d, tn), lambda i, j, l: (l[0], 0, blk_gdn(j))),
            pl.BlockSpec((None, d, LANES), lambda i, j, l: (l[0], 0, 0)),
            pl.BlockSpec((None, CONV_WIDTH, tn), lambda i, j, l: (l[0], 0, jnp.maximum(j - nz, 0))),
            pl.BlockSpec((None, 1, tn), lambda i, j, l: (l[0], 0, jnp.maximum(j - nz, 0))),
            row_p, row_p, col_p, col_p,
        ],
        out_specs=[
            pl.BlockSpec((tm, tn), lambda i, j, l: (i, jnp.minimum(j, nz - 1))),
            colform, colform, colform,
            pl.BlockSpec((LANES, tm), lambda i, j, l: (0, i)),
            pl.BlockSpec((tn // LANES, tm, LANES), lambda i, j, l: (jnp.maximum(j - nz, 0), i, 0)),
        ],
        scratch_shapes=[pltpu.VMEM((tm + 2 * HALO, d), BF16)],
    )
    return pl.pallas_call(
        functools.partial(_inproj_kernel, tm=tm, ni=ni, nz=nz),
        grid_spec=gs,
        out_shape=[jax.ShapeDtypeStruct((seq, COL_CONV), BF16),
                   col_shape, col_shape, col_shape, jax.ShapeDtypeStruct((LANES, seq), F32),
                   jax.ShapeDtypeStruct((N_CONV_BLK, seq, LANES), BF16)],
        compiler_params=_params(("parallel", "arbitrary")),
        name="inproj",
    )(layer, x, x, x, pre_g, w_ssd, w_gdn, w_gate, conv_w, conv_b, brow, arow, bcol, acol)


def _ssd_dir(x_ref, b_ref, c_ref, c1_ref, c2_ref, c3_ref, r2_ref, s_ref, y_ref, *, d):
    q = SSD_CHUNK
    gw = SSD_WIDTH // SSD_GROUPS
    hpg = SSD_HEADS // SSD_GROUPS
    head_shift = SSD_HEAD_DIM.bit_length() - 1
    dt_all = c1_ref[...]
    cs_all = c2_ref[...]
    tot_all = c3_ref[...]
    etot = jnp.exp(tot_all[0:16])
    etot_hi = etot.astype(BF16)
    etot_lo = (etot - etot_hi.astype(F32)).astype(BF16)
    stacked = jnp.concatenate([dt_all.astype(BF16), jnp.exp(cs_all).astype(BF16),
                               jnp.exp(tot_all - cs_all).astype(BF16), etot_hi, etot_lo], axis=0)
    ri = lax.broadcasted_iota(jnp.int32, (q, q), 0)
    ci = lax.broadcasted_iota(jnp.int32, (q, q), 1)
    mask = (ri >= ci) if d == 0 else (ri <= ci)
    kk = lax.broadcasted_iota(jnp.int32, (LANES, gw), 0)
    nn = lax.broadcasted_iota(jnp.int32, (LANES, gw), 1)
    lane = lax.broadcasted_iota(jnp.int32, (q, LANES), 1)
    for g in range(SSD_GROUPS):
        base = G_SSD + SSD_HEADS * d + hpg * g
        sel = (kk == base + (nn >> head_shift)).astype(BF16)
        ex = _mm(stacked, sel)
        dt_e, ecs_e, dte_e = ex[0:q], ex[q:2 * q], ex[2 * q:3 * q]
        etot_e = ex[3 * q:3 * q + 1] + ex[3 * q + 16:3 * q + 17]
        xg = jnp.concatenate([x_ref[gw // LANES * g + p] for p in range(gw // LANES)], axis=1).astype(F32)
        xdt = xg * dt_e
        xdt16 = xdt.astype(BF16)
        xdte16 = (xdt * dte_e).astype(BF16)
        bg = b_ref[g]
        cg = c_ref[g]
        cb = _nt(cg, bg)
        s_prev = s_ref[d, g]
        y_off = _mm(cg, s_prev.astype(BF16)) * ecs_e
        s_ref[d, g] = s_prev * etot_e + _tn(bg, xdte16)
        ys = []
        for pair in range(hpg // 2):
            ms = []
            for hh in (2 * pair, 2 * pair + 1):
                r = base + hh
                dec = jnp.exp(jnp.where(mask, cs_all[:, r:r + 1] - r2_ref[r:r + 1, :], -jnp.inf))
                ms.append((cb * dec).astype(BF16))
            lhs = jnp.concatenate(ms, axis=1)
            xp = xdt16[:, pair * LANES:(pair + 1) * LANES]
            zero = jnp.zeros_like(xp)
            rhs = jnp.concatenate([jnp.where(lane < SSD_HEAD_DIM, xp, zero),
                                   jnp.where(lane >= SSD_HEAD_DIM, xp, zero)], axis=0)
            ys.append(_mm(lhs, rhs))
        y = jnp.concatenate(ys, axis=1) + y_off
        y_ref[:, g * gw:(g + 1) * gw] = y.astype(BF16)


def _ssd_kernel(l_ref,
                xf_ref, bf_ref, cf_ref, c1f_ref, c2f_ref, c3f_ref, r2f_ref,
                xb_ref, bb_ref, cb_ref, c1b_ref, c2b_ref, c3b_ref, r2b_ref,
                yf_ref, yb_ref, s_ref):
    @pl.when(pl.program_id(0) == 0)
    def _():
        s_ref[...] = jnp.zeros_like(s_ref)

    _ssd_dir(xf_ref, bf_ref, cf_ref, c1f_ref, c2f_ref, c3f_ref, r2f_ref, s_ref, yf_ref, d=0)
    _ssd_dir(xb_ref, bb_ref, cb_ref, c1b_ref, c2b_ref, c3b_ref, r2b_ref, s_ref, yb_ref, d=1)


def _ssd_call(layer, act, c1, c2, c3, r2):
    seq = c1.shape[0]
    q = SSD_CHUNK
    nc = seq // q
    nx, nbc = SSD_WIDTH // LANES, SSD_BC // LANES

    def dir_specs(rowblk):
        return [
            pl.BlockSpec((nx, q, LANES), lambda i, l: (ACT_X // nx, rowblk(i), 0)),
            pl.BlockSpec((nbc, q, LANES), lambda i, l: (ACT_B // nbc, rowblk(i), 0)),
            pl.BlockSpec((nbc, q, LANES), lambda i, l: (ACT_C // nbc, rowblk(i), 0)),
            pl.BlockSpec((q, LANES), lambda i, l: (rowblk(i), 0)),
            pl.BlockSpec((q, LANES), lambda i, l: (rowblk(i), 0)),
            pl.BlockSpec((q, LANES), lambda i, l: (rowblk(i), 0)),
            pl.BlockSpec((LANES, q), lambda i, l: (0, rowblk(i))),
        ]

    fwd = lambda i: i
    bwd = lambda i: nc - 1 - i
    gs = pltpu.PrefetchScalarGridSpec(
        num_scalar_prefetch=1,
        grid=(nc,),
        in_specs=dir_specs(fwd) + dir_specs(bwd),
        out_specs=[pl.BlockSpec((q, SSD_WIDTH), lambda i, l: (fwd(i), 0)),
                   pl.BlockSpec((q, SSD_WIDTH), lambda i, l: (bwd(i), 0))],
        scratch_shapes=[pltpu.VMEM((2, SSD_GROUPS, SSD_STATE, SSD_WIDTH // SSD_GROUPS), F32)],
    )
    y_shape = jax.ShapeDtypeStruct((seq, SSD_WIDTH), BF16)
    args = (act, act, act, c1, c2, c3, r2)
    return pl.pallas_call(
        _ssd_kernel,
        grid_spec=gs,
        out_shape=[y_shape, y_shape],
        compiler_params=_params(("arbitrary",)),
        name="ssd",
    )(layer, *args, *args)


def _gdn_prep(q_ref, k_ref, v_ref, c1_ref, c2_ref, c3_ref, r2_ref, *, j, d, head, rows):
    c = GDN_SOLVE_CHUNK
    dh = GDN_HEAD_DIM
    q = q_ref[j, rows, :].astype(F32)
    k = k_ref[j, rows, :].astype(F32)
    v = v_ref[j, rows, :].astype(F32)
    q = q * (lax.rsqrt(jnp.sum(q * q, axis=-1, keepdims=True) + EPS) * (dh ** -0.5))
    k = k * lax.rsqrt(jnp.sum(k * k, axis=-1, keepdims=True) + EPS)

    lane = lax.broadcasted_iota(jnp.int32, (c, LANES), 1)
    ig = G_GDN + GDN_HEADS * d + head
    ib = G_BETA + GDN_HEADS * d + head
    gcs = jnp.sum(jnp.where(lane == ig, c2_ref[rows, :], 0.0), axis=-1, keepdims=True)
    gtot = jnp.sum(jnp.where(lane == ig, c3_ref[rows, :], 0.0), axis=-1, keepdims=True)
    beta = jnp.sum(jnp.where(lane == ib, c1_ref[rows, :], 0.0), axis=-1, keepdims=True)
    grow = r2_ref[pl.ds(ig, 1), :][:, rows]

    ri = lax.broadcasted_iota(jnp.int32, (c, c), 0)
    ci = lax.broadcasted_iota(jnp.int32, (c, c), 1)
    incl = (ri >= ci) if d == 0 else (ri <= ci)
    strict = (ri > ci) if d == 0 else (ri < ci)
    decay = jnp.exp(jnp.where(incl, gcs - grow, -jnp.inf))

    kb = k * beta
    eg = jnp.exp(gcs)
    kq = _nt(jnp.concatenate([kb, q], axis=0).astype(BF16), k.astype(BF16))
    p = jnp.where(strict, -(kq[:c] * decay), 0.0)
    half = c // 2
    left = lax.broadcasted_iota(jnp.int32, (half, LANES), 1) < half
    return dict(
        q=jnp.where(left, p[:half], p[half:]),
        poff16=jnp.where(left, p[half:], p[:half]).astype(BF16),
        attn16=(kq[c:] * decay).astype(BF16),
        rhs16=jnp.concatenate([v * beta, kb * eg], axis=1).astype(BF16),
        qeg=q * eg,
        ket16=(k * jnp.exp(gtot - gcs)).T.astype(BF16),
        ld=jnp.exp(gtot[0:1, :]),
    )


def _gdn_kernel(l_ref,
                qf_ref, kf_ref, vf_ref, c1f_ref, c2f_ref, c3f_ref, r2f_ref,
                qb_ref, kb_ref, vb_ref, c1b_ref, c2b_ref, c3b_ref, r2b_ref,
                of_ref, ob_ref, s_ref, *, hb):
    @pl.when(pl.program_id(1) == 0)
    def _():
        s_ref[...] = jnp.zeros_like(s_ref)

    c = GDN_SOLVE_CHUNK
    dh = GDN_HEAD_DIM
    ncb = GDN_BLOCK // c
    in_refs = ((qf_ref, kf_ref, vf_ref, c1f_ref, c2f_ref, c3f_ref, r2f_ref),
               (qb_ref, kb_ref, vb_ref, c1b_ref, c2b_ref, c3b_ref, r2b_ref))
    out_refs = (of_ref, ob_ref)
    streams = [(j, d) for j in range(hb) for d in range(2)]
    state = {sd: s_ref[sd[1], sd[0]] for sd in streams}

    half = c // 2
    lane_h = lax.broadcasted_iota(jnp.int32, (half, LANES), 1)
    row_h = lax.broadcasted_iota(jnp.int32, (half, LANES), 0)
    left = lane_h < half
    eye2 = (lane_h == row_h) | (lane_h == row_h + half)

    def blockdiag16(x):
        return jnp.concatenate([jnp.where(left, x, 0.0), jnp.where(left, 0.0, x)], axis=0).astype(BF16)

    def scan_read(units):
        for (j, d), u in units.items():
            ws = _mm(u["lhs16"], state[(j, d)].astype(BF16))
            u["v_new16"] = (u["u"] - ws[:c]).astype(BF16)
            u["qs"] = ws[c:]

    def scan_write(units):
        for (j, d), u in units.items():
            state[(j, d)] = state[(j, d)] * u["ld"] + _mm(u["ket16"], u["v_new16"])
            out_refs[d][j, u["rows"], :] = (u["qs"] + _mm(u["attn16"], u["v_new16"])).astype(BF16)

    prev = None
    for phase in range(ncb):
        units = {}
        for (j, d) in streams:
            cc = phase if d == 0 else ncb - 1 - phase
            rows = slice(cc * c, (cc + 1) * c)
            units[(j, d)] = _gdn_prep(*in_refs[d], j=j, d=d, head=pl.program_id(0) * hb + j, rows=rows)
            units[(j, d)]["rows"] = rows
        woven = [] if prev is None else [functools.partial(scan_read, prev), functools.partial(scan_write, prev)]
        for u in units.values():
            u["t"] = jnp.where(eye2, 1.0, u["q"])
            u["q"] = _mm(u["q"].astype(BF16), blockdiag16(u["q"]))
        if woven:
            woven[0]()
        for level in range(1, GDN_INV_SHIFT):
            for u in units.values():
                rhs = blockdiag16(u["q"])
                if level == GDN_INV_SHIFT - 1:
                    u["t"] = u["t"] + _mm(u["t"].astype(BF16), rhs)
                else:
                    qt = _mm(jnp.concatenate([u["q"], u["t"]], axis=0).astype(BF16), rhs)
                    u["q"] = qt[:half]
                    u["t"] = u["t"] + qt[half:]
            if level < len(woven):
                woven[level]()
        for u in units.values():
            u["tbd16"] = blockdiag16(u["t"])
            u["y16"] = _mm(u["poff16"], u["tbd16"]).astype(BF16)
        zeros = jnp.zeros((half, LANES), BF16)
        for (j, d), u in units.items():
            tbd16 = u["tbd16"]
            if d == 0:
                z = _mm(tbd16[half:], jnp.concatenate([zeros, u["y16"]], axis=0))
                u["t16"] = jnp.concatenate([tbd16[:half], jnp.where(left, z, u["t"]).astype(BF16)], axis=0)
            else:
                z = _mm(tbd16[:half], jnp.concatenate([u["y16"], zeros], axis=0))
                u["t16"] = jnp.concatenate([jnp.where(left, u["t"], z).astype(BF16), tbd16[half:]], axis=0)
        for u in units.values():
            sol = _mm(u["t16"], u["rhs16"])
            u["u"] = sol[:, :dh]
            u["lhs16"] = jnp.concatenate([sol[:, dh:], u["qeg"]], axis=0).astype(BF16)
        prev = units
    scan_read(prev)
    scan_write(prev)
    for (j, d) in streams:
        s_ref[d, j] = state[(j, d)]


def _gdn_call(layer, act, c1, c2, c3, r2, *, hb=8):
    seq = c1.shape[0]
    r = GDN_BLOCK
    nb = seq // r
    q0, k0, v0 = ACT_Q // hb, ACT_K // hb, ACT_V // hb

    def dir_specs(rowblk):
        return [
            pl.BlockSpec((hb, r, LANES), lambda h, i, l: (q0 + h, rowblk(i), 0)),
            pl.BlockSpec((hb, r, LANES), lambda h, i, l: (k0 + h, rowblk(i), 0)),
            pl.BlockSpec((hb, r, LANES), lambda h, i, l: (v0 + h, rowblk(i), 0)),
            pl.BlockSpec((r, LANES), lambda h, i, l: (rowblk(i), 0)),
            pl.BlockSpec((r, LANES), lambda h, i, l: (rowblk(i), 0)),
            pl.BlockSpec((r, LANES), lambda h, i, l: (rowblk(i), 0)),
            pl.BlockSpec((LANES, r), lambda h, i, l: (0, rowblk(i))),
        ]

    fwd = lambda i: i
    bwd = lambda i: nb - 1 - i
    gs = pltpu.PrefetchScalarGridSpec(
        num_scalar_prefetch=1,
        grid=(GDN_HEADS // hb, nb),
        in_specs=dir_specs(fwd) + dir_specs(bwd),
        out_specs=[pl.BlockSpec((hb, r, LANES), lambda h, i, l: (h, fwd(i), 0)),
                   pl.BlockSpec((hb, r, LANES), lambda h, i, l: (h, bwd(i), 0))],
        scratch_shapes=[pltpu.VMEM((2, hb, GDN_HEAD_DIM, GDN_HEAD_DIM), F32)],
    )
    o_shape = jax.ShapeDtypeStruct((GDN_HEADS, seq, LANES), BF16)
    args = (act, act, act, c1, c2, c3, r2)
    return pl.pallas_call(
        functools.partial(_gdn_kernel, hb=hb),
        grid_spec=gs,
        out_shape=[o_shape, o_shape],
        compiler_params=_params(("parallel", "arbitrary")),
        name="gdn",
    )(layer, *args, *args)


def _outproj_kernel(l_ref, x_ref, yf_ref, yb_ref, xa_ref, zs_ref, zg_ref, of_ref, ob_ref,
                    dexp_ref, sg_ref, gg_ref, w_ref, post_ref, o_ref):
    gw = SSD_WIDTH // SSD_GROUPS
    hq = gw // LANES
    acc = None
    for g in range(SSD_GROUPS):
        sl = slice(g * gw, (g + 1) * gw)
        xg = jnp.concatenate([xa_ref[hq * g + p] for p in range(hq)], axis=1).astype(F32)
        y = yf_ref[:, sl].astype(F32) + yb_ref[:, sl].astype(F32) + dexp_ref[:, sl] * xg
        y = y * _silu(zs_ref[:, sl].astype(F32))
        part = _mm(_rms(y, sg_ref[:, sl]).astype(BF16), w_ref[sl, :])
        acc = part if acc is None else acc + part
    for g in range(GDN_WIDTH // gw):
        os = []
        for h in range(hq * g, hq * (g + 1)):
            o = of_ref[h].astype(F32) + ob_ref[h].astype(F32)
            o = _rms(o, gg_ref[...]) * _silu(zg_ref[:, h * LANES:(h + 1) * LANES].astype(F32))
            os.append(o.astype(BF16))
        acc = acc + _mm(jnp.concatenate(os, axis=1), w_ref[SSD_WIDTH + g * gw:SSD_WIDTH + (g + 1) * gw, :])
    o_ref[...] = x_ref[...] + _rms(acc, post_ref[...])


def _outproj_call(layer, x, yf, yb, act, z, of, ob, dexp, ssd_g, gdn_g, w_out, post_g, *, tm=256):
    seq, d = x.shape
    row_d = pl.BlockSpec((None, 1, d), lambda i, l: (l[0], 0, 0))
    gs = pltpu.PrefetchScalarGridSpec(
        num_scalar_prefetch=1,
        grid=(seq // tm,),
        in_specs=[
            pl.BlockSpec((tm, d), lambda i, l: (i, 0)),
            pl.BlockSpec((tm, SSD_WIDTH), lambda i, l: (i, 0)),
            pl.BlockSpec((tm, SSD_WIDTH), lambda i, l: (i, 0)),
            pl.BlockSpec((SSD_WIDTH // LANES, tm, LANES), lambda i, l: (ACT_X, i, 0)),
            pl.BlockSpec((tm, SSD_WIDTH), lambda i, l: (i, COL_ZS // SSD_WIDTH)),
            pl.BlockSpec((tm, GDN_WIDTH), lambda i, l: (i, COL_ZG // GDN_WIDTH)),
            pl.BlockSpec((GDN_HEADS, tm, LANES), lambda i, l: (0, i, 0)),
            pl.BlockSpec((GDN_HEADS, tm, LANES), lambda i, l: (0, i, 0)),
            row_d, row_d,
            pl.BlockSpec((None, 1, LANES), lambda i, l: (l[0], 0, 0)),
            pl.BlockSpec((None, 2 * d, d), lambda i, l: (l[0], 0, 0), pipeline_mode=pl.Buffered(1)),
            row_d,
        ],
        out_specs=pl.BlockSpec((tm, d), lambda i, l: (i, 0)),
    )
    return pl.pallas_call(
        _outproj_kernel,
        grid_spec=gs,
        out_shape=jax.ShapeDtypeStruct((seq, d), F32),
        compiler_params=_params(("parallel",)),
        name="outproj",
    )(layer, x, yf, yb, act, z, z, of, ob, dexp, ssd_g, gdn_g, w_out, post_g)


def kernel(x, ffn1_pre_g, ffn1_w_gu, ffn1_w_down, ffn1_post_g, mix_pre_g, w_in, ssd_conv_w, ssd_conv_b,
           ssd_dt_bias, ssd_a_log, ssd_d, ssd_norm_g, gdn_conv_w, gdn_dt_bias, gdn_a_log, gdn_norm_g, w_out,
           mix_post_g, ffn2_pre_g, ffn2_w_gu, ffn2_w_down, ffn2_post_g):
    row = lambda g: g.astype(F32).reshape(DEPTH, 1, -1)

    s0, s1, s2 = SSD_WIDTH, SSD_WIDTH + SSD_CONV_CH, SSD_WIDTH + SSD_CONV_CH + 2 * SSD_HEADS
    g1, g2 = s2 + GDN_CONV_CH, s2 + GDN_CONV_CH + GDN_WIDTH
    w_ssd = w_in[..., :s1].astype(BF16)
    w_gdn = w_in[..., s2:g2].astype(BF16)
    w_gate = jnp.concatenate([w_in[..., s1:s2], w_in[..., g2:]], axis=-1).astype(BF16)
    conv_w = jnp.concatenate([ssd_conv_w, gdn_conv_w], axis=-1).astype(F32)
    conv_b = jnp.concatenate([ssd_conv_b.astype(F32), jnp.zeros((DEPTH, GDN_CONV_CH), F32)], axis=-1)[:, None, :]
    pad = jnp.zeros((DEPTH, LANES - G_BETA), F32)
    bias = jnp.concatenate([ssd_dt_bias.reshape(DEPTH, -1), gdn_dt_bias.reshape(DEPTH, -1), pad], axis=-1)
    alog = jnp.concatenate([ssd_a_log.reshape(DEPTH, -1), gdn_a_log.reshape(DEPTH, -1), pad], axis=-1)
    brow, arow = bias[:, None, :], alog[:, None, :]
    bcol, acol = bias[:, :, None], alog[:, :, None]
    dexp = jnp.repeat(ssd_d.astype(F32), SSD_HEAD_DIM, axis=-1)[:, None, :]
    f1_gu, f1_dn = ffn1_w_gu.astype(BF16), ffn1_w_down.astype(BF16)
    f2_gu, f2_dn = ffn2_w_gu.astype(BF16), ffn2_w_down.astype(BF16)
    w_out16 = w_out.astype(BF16)
    f1_pre, f1_post, f2_pre, f2_post = row(ffn1_pre_g), row(ffn1_post_g), row(ffn2_pre_g), row(ffn2_post_g)
    m_pre, m_post, ssd_g, gdn_g = row(mix_pre_g), row(mix_post_g), row(ssd_norm_g), row(gdn_norm_g)

    def layer_fn(i, xc):
        layer = jnp.reshape(i, (1,)).astype(jnp.int32)
        xc = _ffn_call(layer, xc, f1_pre, f1_gu, f1_dn, f1_post)
        z, c1, c2, c3, r2, act = _inproj_call(layer, xc, m_pre, w_ssd, w_gdn, w_gate, conv_w, conv_b, brow, arow, bcol, acol)
        yf, yb = _ssd_call(layer, act, c1, c2, c3, r2)
        of, ob = _gdn_call(layer, act, c1, c2, c3, r2)
        xc = _outproj_call(layer, xc, yf, yb, act, z, of, ob, dexp, ssd_g, gdn_g, w_out16, m_post)
        xc = _ffn_call(layer, xc, f2_pre, f2_gu, f2_dn, f2_post)
        return xc

    out = lax.fori_loop(0, DEPTH, layer_fn, x.reshape(SEQ, D_MODEL).astype(F32))
    return out.reshape(x.shape)
```
